```python
import math
import jax, jax.numpy as jnp
from jax import lax
import numpy as np


D_MODEL = 1024
BATCH = 8
SEQ = 8192
DEPTH = 1

MEM_TOKENS = 256
S5_WIDTH = D_MODEL // 4
S5_GROUP_CH = 16
S5_GROUPS = S5_WIDTH // S5_GROUP_CH
S5_STATE = 64
S5_MAX_RE = -1e-4
S5_DT_MIN = 1e-3
S5_DT_MAX = 1e-1
MLA_HEADS = 8
MLA_NOPE_DIM = 64
MLA_ROPE_DIM = 32
MLA_QK_DIM = MLA_NOPE_DIM + MLA_ROPE_DIM
MLA_V_DIM = 64
MLA_Q_RANK = D_MODEL // 4
MLA_KV_RANK = D_MODEL // 4
ROPE_THETA = 10000.0
Q_BLOCK = 128
XATTN_HEADS = 4
XATTN_HEAD_DIM = D_MODEL // XATTN_HEADS
MLP_HIDDEN = 4 * D_MODEL
LN_EPS = 1e-5
RMS_EPS = 1e-6
NEG_INF = -1e30
POS_OFFSET_MAX = 4096
DN_ALPHA = (2.0 * DEPTH) ** 0.25
DN_BETA = (8.0 * DEPTH) ** -0.25
IN_S5 = S5_WIDTH
IN_Q = MLA_Q_RANK
IN_KV = MLA_KV_RANK
IN_KR = MLA_ROPE_DIM
IN_GATE = 2 * D_MODEL
IN_WIDTH = IN_S5 + IN_Q + IN_KV + IN_KR + IN_GATE

kernel_name = "hybrid_s5_mla_gated_deepnorm_layer"


def layer_norm(x, g, b):
    xf = x.astype(jnp.float32)
    mu = jnp.mean(xf, axis=-1, keepdims=True)
    xc = xf - mu
    var = jnp.mean(xc * xc, axis=-1, keepdims=True)
    return (xc * lax.rsqrt(var + LN_EPS) * g.astype(jnp.float32) + b.astype(jnp.float32)).astype(x.dtype)


def rms_norm(x, g):
    xf = x.astype(jnp.float32)
    return (xf * lax.rsqrt(jnp.mean(xf * xf, axis=-1, keepdims=True) + RMS_EPS) * g.astype(jnp.float32)).astype(x.dtype)


def rope_tables(positions):
    inv = ROPE_THETA ** (-jnp.arange(0, MLA_ROPE_DIM, 2, dtype=jnp.float32) / MLA_ROPE_DIM)
    ang = positions.astype(jnp.float32)[..., None] * inv
    return jnp.cos(ang)[:, :, None, :], jnp.sin(ang)[:, :, None, :]


def apply_rope(x, cos, sin):
    xf = x.astype(jnp.float32)
    x1, x2 = jnp.split(xf, 2, axis=-1)
    return jnp.concatenate([x1 * cos - x2 * sin, x1 * sin + x2 * cos], axis=-1).astype(x.dtype)


def _complex_scan_op(e1, e2):
    a1r, a1i, b1r, b1i = e1
    a2r, a2i, b2r, b2i = e2
    ar = a1r * a2r - a1i * a2i
    ai = a1r * a2i + a1i * a2r
    br = a2r * b1r - a2i * b1i + b2r
    bi = a2r * b1i + a2i * b1r + b2i
    return (ar, ai, br, bi)


def s5_ssm(u, lam_re, lam_im, log_dt, b_re, b_im, c_re, c_im, d_skip):
    bsz, seq, _ = u.shape
    uf = u.astype(jnp.float32).reshape(bsz, seq, S5_GROUPS, S5_GROUP_CH)
    lr = jnp.minimum(lam_re.astype(jnp.float32), S5_MAX_RE)
    li = lam_im.astype(jnp.float32)
    dt = jnp.exp(log_dt.astype(jnp.float32))[:, None]
    mag = jnp.exp(lr * dt)
    ang = li * dt
    ab_re = mag * jnp.cos(ang)
    ab_im = mag * jnp.sin(ang)
    den = lr * lr + li * li
    nr = ab_re - 1.0
    f_re = ((nr * lr + ab_im * li) / den)[..., None]
    f_im = ((ab_im * lr - nr * li) / den)[..., None]
    br = b_re.astype(jnp.float32)
    bi = b_im.astype(jnp.float32)
    bb_re = f_re * br - f_im * bi
    bb_im = f_re * bi + f_im * br
    bu_re = jnp.einsum('bsgh,gph->bsgp', uf, bb_re)
    bu_im = jnp.einsum('bsgh,gph->bsgp', uf, bb_im)
    a_re = jnp.broadcast_to(ab_re[None, None], (1, seq, S5_GROUPS, S5_STATE))
    a_im = jnp.broadcast_to(ab_im[None, None], (1, seq, S5_GROUPS, S5_STATE))
    _, _, h_re, h_im = lax.associative_scan(_complex_scan_op, (a_re, a_im, bu_re, bu_im), axis=1)
    y = (jnp.einsum('bsgp,ghp->bsgh', h_re, c_re.astype(jnp.float32))
         - jnp.einsum('bsgp,ghp->bsgh', h_im, c_im.astype(jnp.float32)))
    y = y + d_skip.astype(jnp.float32).reshape(S5_GROUPS, S5_GROUP_CH) * uf
    return y.reshape(bsz, seq, S5_WIDTH)


def causal_block_attention(q, k, v):
    bsz, seq, heads, dqk = q.shape
    nblk = seq // Q_BLOCK
    scale = dqk ** -0.5
    qb = jnp.moveaxis(q.reshape(bsz, nblk, Q_BLOCK, heads, dqk), 1, 0)
    starts = jnp.arange(nblk, dtype=jnp.int32) * Q_BLOCK
    kpos = jnp.arange(seq, dtype=jnp.int32)

    def block(args):
        qi, start = args
        s = jnp.einsum('bqhd,bkhd->bhqk', qi, k).astype(jnp.float32) * scale
        qpos = start + jnp.arange(Q_BLOCK, dtype=jnp.int32)
        s = jnp.where(kpos[None, :] <= qpos[:, None], s, NEG_INF)
        p = jax.nn.softmax(s, axis=-1).astype(v.dtype)
        return jnp.einsum('bhqk,bkhd->bqhd', p, v)

    o = lax.map(block, (qb, starts))
    return jnp.moveaxis(o, 0, 1).reshape(bsz, seq, heads * v.shape[-1])


def hybrid_mixer(h, cos, sin, w_in, s5_lam_re, s5_lam_im, s5_log_dt, s5_b_re, s5_b_im,
                 s5_c_re, s5_c_im, s5_d, w_glu, q_norm_g, w_uq, kv_norm_g, w_ukv, w_oa, w_o):
    bsz, seq, _ = h.shape
    z = h @ w_in
    o1 = IN_S5
    o2 = o1 + IN_Q
    o3 = o2 + IN_KV
    o4 = o3 + IN_KR
    u = z[..., :o1]
    c_q = z[..., o1:o2]
    c_kv = z[..., o2:o3]
    k_r = z[..., o3:o4]
    gate = z[..., o4:]
    y = s5_ssm(u, s5_lam_re, s5_lam_im, s5_log_dt, s5_b_re, s5_b_im, s5_c_re, s5_c_im, s5_d).astype(h.dtype)
    y = jax.nn.gelu(y, approximate=False) @ w_glu
    s_out = y[..., :D_MODEL] * jax.nn.sigmoid(y[..., D_MODEL:])
    q = (rms_norm(c_q, q_norm_g) @ w_uq).reshape(bsz, seq, MLA_HEADS, MLA_QK_DIM)
    q = jnp.concatenate([q[..., :MLA_NOPE_DIM], apply_rope(q[..., MLA_NOPE_DIM:], cos, sin)], axis=-1)
    kv = (rms_norm(c_kv, kv_norm_g) @ w_ukv).reshape(bsz, seq, MLA_HEADS, MLA_NOPE_DIM + MLA_V_DIM)
    k_rope = apply_rope(k_r[:, :, None, :], cos, sin)
    k = jnp.concatenate([kv[..., :MLA_NOPE_DIM],
                         jnp.broadcast_to(k_rope, (bsz, seq, MLA_HEADS, MLA_ROPE_DIM))], axis=-1)
    v = kv[..., MLA_NOPE_DIM:]
    a_out = causal_block_attention(q, k, v) @ w_oa
    g_s = jax.nn.sigmoid(gate[..., :D_MODEL])
    g_a = jax.nn.sigmoid(gate[..., D_MODEL:])
    return (g_s * s_out + g_a * a_out) @ w_o


def memory_cross_attention(h, mem, w_xq, w_xk, w_xv, w_xo):
    bsz, seq, _ = h.shape
    m = mem.shape[1]
    q = (h @ w_xq).reshape(bsz, seq, XATTN_HEADS, XATTN_HEAD_DIM)
    k = (mem @ w_xk).reshape(bsz, m, XATTN_HEADS, XATTN_HEAD_DIM)
    v = (mem @ w_xv).reshape(bsz, m, XATTN_HEADS, XATTN_HEAD_DIM)
    s = jnp.einsum('bshd,bmhd->bhsm', q, k).astype(jnp.float32) * (XATTN_HEAD_DIM ** -0.5)
    p = jax.nn.softmax(s, axis=-1).astype(v.dtype)
    o = jnp.einsum('bhsm,bmhd->bshd', p, v).reshape(bsz, seq, D_MODEL)
    return o @ w_xo


def squared_relu_mlp(h, w_up, w_down):
    return jnp.square(jax.nn.relu(h @ w_up)) @ w_down


def setup_inputs(seed: int = 0) -> dict:
    key = jax.random.key(seed)
    ks = jax.random.split(key, 40)
    f32 = jnp.float32

    def nrm(k, shape, scale):
        return jax.random.normal(k, shape, f32) * scale

    L = DEPTH
    G, P, H = S5_GROUPS, S5_STATE, S5_GROUP_CH
    n = jnp.arange(P, dtype=f32)
    positions = (jax.random.randint(ks[2], (BATCH, 1), 0, POS_OFFSET_MAX, dtype=jnp.int32)
                 + jnp.arange(SEQ, dtype=jnp.int32)[None, :])
    return {
        "x": nrm(ks[0], (BATCH, SEQ, D_MODEL), 1.0),
        "mem": nrm(ks[1], (BATCH, MEM_TOKENS, D_MODEL), 1.0),
        "positions": positions,
        "ln_in_g": 1.0 + nrm(ks[3], (D_MODEL,), 0.02),
        "ln_in_b": nrm(ks[4], (D_MODEL,), 0.02),
        "w_in": nrm(ks[5], (L, D_MODEL, IN_WIDTH), D_MODEL ** -0.5),
        "s5_lam_re": -0.5 + nrm(ks[6], (L, G, P), 0.01),
        "s5_lam_im": math.pi * n + nrm(ks[7], (L, G, P), 0.01),
        "s5_log_dt": jax.random.uniform(ks[8], (L, G), f32, math.log(S5_DT_MIN), math.log(S5_DT_MAX)),
        "s5_b_re": nrm(ks[9], (L, G, P, H), (2.0 * H) ** -0.5),
        "s5_b_im": nrm(ks[10], (L, G, P, H), (2.0 * H) ** -0.5),
        "s5_c_re": nrm(ks[11], (L, G, H, P), P ** -0.5),
        "s5_c_im": nrm(ks[12], (L, G, H, P), P ** -0.5),
        "s5_d": nrm(ks[13], (L, S5_WIDTH), 1.0),
        "w_glu": nrm(ks[14], (L, S5_WIDTH, 2 * D_MODEL), S5_WIDTH ** -0.5),
        "q_norm_g": 1.0 + nrm(ks[15], (L, MLA_Q_RANK), 0.02),
        "w_uq": nrm(ks[16], (L, MLA_Q_RANK, MLA_HEADS * MLA_QK_DIM), MLA_Q_RANK ** -0.5),
        "kv_norm_g": 1.0 + nrm(ks[17], (L, MLA_KV_RANK), 0.02),
        "w_ukv": nrm(ks[18], (L, MLA_KV_RANK, MLA_HEADS * (MLA_NOPE_DIM + MLA_V_DIM)), MLA_KV_RANK ** -0.5),
        "w_oa": nrm(ks[19], (L, MLA_HEADS * MLA_V_DIM, D_MODEL), (MLA_HEADS * MLA_V_DIM) ** -0.5),
        "w_o": nrm(ks[20], (L, D_MODEL, D_MODEL), DN_BETA * D_MODEL ** -0.5),
        "ln1_g": 1.0 + nrm(ks[21], (L, D_MODEL), 0.02),
        "ln1_b": nrm(ks[22], (L, D_MODEL), 0.02),
        "w_xq": nrm(ks[23], (L, D_MODEL, D_MODEL), D_MODEL ** -0.5),
        "w_xk": nrm(ks[24], (L, D_MODEL, D_MODEL), D_MODEL ** -0.5),
        "w_xv": nrm(ks[25], (L, D_MODEL, D_MODEL), DN_BETA * D_MODEL ** -0.5),
        "w_xo": nrm(ks[26], (L, D_MODEL, D_MODEL), DN_BETA * D_MODEL ** -0.5),
        "ln2_g": 1.0 + nrm(ks[27], (L, D_MODEL), 0.02),
        "ln2_b": nrm(ks[28], (L, D_MODEL), 0.02),
        "w_up": nrm(ks[29], (L, D_MODEL, MLP_HIDDEN), DN_BETA * D_MODEL ** -0.5),
        "w_down": nrm(ks[30], (L, MLP_HIDDEN, D_MODEL), DN_BETA * MLP_HIDDEN ** -0.5),
        "ln3_g": 1.0 + nrm(ks[31], (L, D_MODEL), 0.02),
        "ln3_b": nrm(ks[32], (L, D_MODEL), 0.02),
    }


def reference(x, mem, positions, ln_in_g, ln_in_b, w_in, s5_lam_re, s5_lam_im, s5_log_dt,
              s5_b_re, s5_b_im, s5_c_re, s5_c_im, s5_d, w_glu, q_norm_g, w_uq, kv_norm_g, w_ukv,
              w_oa, w_o, ln1_g, ln1_b, w_xq, w_xk, w_xv, w_xo, ln2_g, ln2_b, w_up, w_down,
              ln3_g, ln3_b):
    cos, sin = rope_tables(positions)
    h = layer_norm(x, ln_in_g, ln_in_b)
    for l in range(DEPTH):
        mix = hybrid_mixer(h, cos, sin, w_in[l], s5_lam_re[l], s5_lam_im[l], s5_log_dt[l],
                           s5_b_re[l], s5_b_im[l], s5_c_re[l], s5_c_im[l], s5_d[l], w_glu[l],
                           q_norm_g[l], w_uq[l], kv_norm_g[l], w_ukv[l], w_oa[l], w_o[l])
        h = layer_norm(DN_ALPHA * h + mix, ln1_g[l], ln1_b[l])
        xa = memory_cross_attention(h, mem, w_xq[l], w_xk[l], w_xv[l], w_xo[l])
        h = layer_norm(DN_ALPHA * h + xa, ln2_g[l], ln2_b[l])
        ff = squared_relu_mlp(h, w_up[l], w_down[l])
        h = layer_norm(DN_ALPHA * h + ff, ln3_g[l], ln3_b[l])
    return h
```

```python
import functools
import math

import jax
import jax.numpy as jnp
from jax import lax
from jax.experimental import pallas as pl
from jax.experimental.pallas import tpu as pltpu

F32 = jnp.float32
BF16 = jnp.bfloat16

S5_GROUP_CH = 16
S5_STATE = 64
S5_MAX_RE = -1e-4
MLA_HEADS = 8
MLA_NOPE_DIM = 64
MLA_ROPE_DIM = 32
MLA_QK_DIM = MLA_NOPE_DIM + MLA_ROPE_DIM
MLA_V_DIM = 64
ROPE_THETA = 10000.0
XATTN_HEADS = 4
LN_EPS = 1e-5
RMS_EPS = 1e-6
NEG_INF = -1e30
DEPTH = 1
DN_ALPHA = (2.0 * DEPTH) ** 0.25

LANE = 128
VMEM_LIMIT = 56 * 1024 * 1024


def _const_spec(shape):
    nd = len(shape)
    return pl.BlockSpec(shape, lambda *_: (0,) * nd, pipeline_mode=pl.Buffered(1))


def _params(*sem):
    return pltpu.CompilerParams(dimension_semantics=sem, vmem_limit_bytes=VMEM_LIMIT)


def _layer_norm(x, g, b):
    mu = jnp.mean(x, axis=-1, keepdims=True)
    xc = x - mu
    var = jnp.mean(xc * xc, axis=-1, keepdims=True)
    return xc * lax.rsqrt(var + LN_EPS) * g + b


def _rms_norm(x, g):
    return x * lax.rsqrt(jnp.mean(x * x, axis=-1, keepdims=True) + RMS_EPS) * g


def _gelu_exact(x):
    return 0.5 * x * (1.0 + lax.erf(x * (0.5 ** 0.5)))


def _dot(a, b):
    return jnp.dot(a, b, preferred_element_type=F32)


def _dot_nt(a, b):
    return lax.dot_general(a, b, (((1,), (1,)), ((), ())), preferred_element_type=F32)


def _s5_disc_kernel(lr_ref, li_ref, ldt_ref, br_ref, bi_ref, are_ref, aim_ref, bbr_ref, bbi_ref):
    lr = jnp.minimum(lr_ref[...], S5_MAX_RE)
    li = li_ref[...]
    dt = jnp.exp(ldt_ref[...])
    mag = jnp.exp(lr * dt)
    ang = li * dt
    ab_re = mag * jnp.cos(ang)
    ab_im = mag * jnp.sin(ang)
    den = lr * lr + li * li
    nr = ab_re - 1.0
    f_re = (nr * lr + ab_im * li) / den
    f_im = (ab_im * lr - nr * li) / den
    br = br_ref[...]
    bi = bi_ref[...]
    are_ref[...] = ab_re
    aim_ref[...] = ab_im
    bbr_ref[...] = f_re * br - f_im * bi
    bbi_ref[...] = f_re * bi + f_im * br


def _s5_discretize(lam_re, lam_im, log_dt, b_re, b_im):
    g, p = lam_re.shape
    h = b_re.shape[-1]
    n = g * p
    col = lambda a: a.reshape(n, 1)
    ldt = jnp.broadcast_to(log_dt[:, None], (g, p))
    outs = pl.pallas_call(
        _s5_disc_kernel,
        out_shape=[jax.ShapeDtypeStruct((n, 1), F32), jax.ShapeDtypeStruct((n, 1), F32),
                   jax.ShapeDtypeStruct((n, h), F32), jax.ShapeDtypeStruct((n, h), F32)],
        name="s5_discretize",
    )(col(lam_re), col(lam_im), col(ldt), b_re.reshape(n, h), b_im.reshape(n, h))
    ab_re, ab_im, bb_re, bb_im = outs
    return ab_re.reshape(g, p), ab_im.reshape(g, p), bb_re.reshape(g, p, h), bb_im.reshape(g, p, h)


def _rope_kernel(pos_ref, inv_ref, cos_ref, sin_ref):
    ang = pos_ref[...].astype(F32) * inv_ref[...]
    cos_ref[...] = jnp.cos(ang)
    sin_ref[...] = jnp.sin(ang)


def _rope_tables(positions):
    bsz, seq = positions.shape
    half = MLA_ROPE_DIM // 2
    per_row = LANE // half
    rows = bsz * seq // per_row
    inv = ROPE_THETA ** (-jnp.arange(0, MLA_ROPE_DIM, 2, dtype=F32) / MLA_ROPE_DIM)
    inv_row = jnp.tile(inv, per_row).reshape(1, LANE)
    pos_rep = jnp.repeat(positions.reshape(-1), half).reshape(rows, LANE)
    tr = min(rows, 1024)
    cos, sin = pl.pallas_call(
        _rope_kernel,
        grid=(rows // tr,),
        in_specs=[pl.BlockSpec((tr, LANE), lambda i: (i, 0)), _const_spec((1, LANE))],
        out_specs=[pl.BlockSpec((tr, LANE), lambda i: (i, 0))] * 2,
        out_shape=[jax.ShapeDtypeStruct((rows, LANE), F32)] * 2,
        compiler_params=_params("parallel"),
        name="rope_table",
    )(pos_rep, inv_row)
    return cos.reshape(bsz, seq, half), sin.reshape(bsz, seq, half)


def _mem_kv_kernel(mem_ref, wk_ref, wv_ref, k_ref, v_ref):
    m = mem_ref[...].astype(BF16)
    k_ref[...] = _dot(m, wk_ref[...]).astype(BF16)
    v_ref[...] = _dot(m, wv_ref[...]).astype(BF16)


def _mem_kv(mem, w_xk, w_xv):
    bsz, m, d = mem.shape
    return pl.pallas_call(
        _mem_kv_kernel,
        grid=(bsz,),
        in_specs=[pl.BlockSpec((None, m, d), lambda b: (b, 0, 0)), _const_spec((d, d)), _const_spec((d, d))],
        out_specs=[pl.BlockSpec((None, m, d), lambda b: (b, 0, 0))] * 2,
        out_shape=[jax.ShapeDtypeStruct((bsz, m, d), BF16)] * 2,
        compiler_params=_params("parallel"),
        name="mem_kv",
    )(mem, w_xk.astype(BF16), w_xv.astype(BF16))


def _in_proj_kernel(x_ref, cos_ref, sin_ref, lng_ref, lnb_ref, w_u_ref, w_cq_ref, w_ckv_ref, w_kr_ref,
                    w_g_ref, qg_ref, kvg_ref, w_q_ref, w_qs_ref, w_k_ref, w_v_ref,
                    u_ref, q_ref, k_ref, v_ref, gate_ref, *, q_scale):
    hb = _layer_norm(x_ref[...], lng_ref[...], lnb_ref[...]).astype(BF16)
    cos = cos_ref[...]
    sin = sin_ref[...]
    u_ref[...] = _dot(hb, w_u_ref[...])
    gate_ref[...] = jax.nn.sigmoid(_dot(hb, w_g_ref[...])).astype(BF16)
    kr2 = _dot(hb, w_kr_ref[...])
    k_rope = kr2[:, :LANE] * cos + kr2[:, LANE:] * sin
    cqn = _rms_norm(_dot(hb, w_cq_ref[...]), qg_ref[...]).astype(BF16)
    qa = _dot(cqn, w_q_ref[...])
    qb = _dot(cqn, w_qs_ref[...])
    ckvn = _rms_norm(_dot(hb, w_ckv_ref[...]), kvg_ref[...]).astype(BF16)
    kn = _dot(ckvn, w_k_ref[...])
    v_ref[...] = _dot(ckvn, w_v_ref[...]).astype(BF16)
    for h in range(MLA_HEADS):
        sl = slice(h * LANE, (h + 1) * LANE)
        q_ref[:, sl] = ((qa[:, sl] * cos + qb[:, sl] * sin) * q_scale).astype(BF16)
        k_ref[:, sl] = (kn[:, sl] + k_rope).astype(BF16)


def _rotate_half_cols(w):
    half = MLA_ROPE_DIM // 2
    return jnp.concatenate([-w[..., half:], w[..., :half]], axis=-1)


def _head_slab(nope, rope):
    d, h, _ = nope.shape
    pad = jnp.zeros((d, h, LANE - MLA_QK_DIM), nope.dtype)
    return jnp.concatenate([nope, rope, pad], axis=-1).reshape(d, h * LANE)


def _in_proj(x2, cos128, sin128, ln_g, ln_b, w_in, q_norm_g, w_uq, kv_norm_g, w_ukv, tm):
    n, d = x2.shape
    s5w = q_rank = kv_rank = d // 4
    o1, o2, o3 = s5w, s5w + q_rank, s5w + q_rank + kv_rank
    o4 = o3 + MLA_ROPE_DIM
    hl = MLA_HEADS * LANE
    w_u = w_in[:, :o1].astype(BF16)
    w_cq = w_in[:, o1:o2].astype(BF16)
    w_ckv = w_in[:, o2:o3].astype(BF16)
    w_kr = w_in[:, o3:o4]
    zl = jnp.zeros((d, MLA_NOPE_DIM), F32)
    zr = jnp.zeros((d, LANE - MLA_QK_DIM), F32)
    w_kr2 = jnp.concatenate([zl, w_kr, zr, zl, _rotate_half_cols(w_kr), zr], axis=-1).astype(BF16)
    w_g = w_in[:, o4:].astype(BF16)
    wq3 = w_uq.reshape(q_rank, MLA_HEADS, MLA_QK_DIM)
    wq_nope, wq_rope = wq3[..., :MLA_NOPE_DIM], wq3[..., MLA_NOPE_DIM:]
    w_q = _head_slab(wq_nope, wq_rope).astype(BF16)
    w_qs = _head_slab(jnp.zeros_like(wq_nope), _rotate_half_cols(wq_rope)).astype(BF16)
    wkv3 = w_ukv.reshape(kv_rank, MLA_HEADS, MLA_NOPE_DIM + MLA_V_DIM)
    w_k = _head_slab(wkv3[..., :MLA_NOPE_DIM],
                     jnp.zeros((kv_rank, MLA_HEADS, MLA_ROPE_DIM), F32)).astype(BF16)
    w_v = wkv3[..., MLA_NOPE_DIM:].reshape(kv_rank, MLA_HEADS * MLA_V_DIM).astype(BF16)
    row = lambda w: pl.BlockSpec((tm, w), lambda i: (i, 0))
    consts = [ln_g.reshape(1, d), ln_b.reshape(1, d), w_u, w_cq, w_ckv, w_kr2, w_g,
              q_norm_g.reshape(1, q_rank), kv_norm_g.reshape(1, kv_rank), w_q, w_qs, w_k, w_v]
    return pl.pallas_call(
        functools.partial(_in_proj_kernel, q_scale=MLA_QK_DIM ** -0.5),
        grid=(n // tm,),
        in_specs=[row(d), row(LANE), row(LANE)] + [_const_spec(c.shape) for c in consts],
        out_specs=[row(s5w), row(hl), row(hl), row(MLA_HEADS * MLA_V_DIM), row(2 * d)],
        out_shape=[jax.ShapeDtypeStruct((n, s5w), F32), jax.ShapeDtypeStruct((n, hl), BF16),
                   jax.ShapeDtypeStruct((n, hl), BF16),
                   jax.ShapeDtypeStruct((n, MLA_HEADS * MLA_V_DIM), BF16),
                   jax.ShapeDtypeStruct((n, 2 * d), BF16)],
        compiler_params=_params("parallel"),
        name="in_proj",
    )(x2, cos128, sin128, *consts)


def _s5_scan_kernel(u_ref, bblk_ref, are_ref, aim_ref, cblk_ref, d_ref, wglu_ref, out_ref,
                    hbuf, st_re, st_im, *, bsz, tt, nstate):
    @pl.when(pl.program_id(0) == 0)
    def _():
        st_re[...] = jnp.zeros_like(st_re)
        st_im[...] = jnp.zeros_like(st_im)

    width = u_ref.shape[-1]
    u = u_ref[...].reshape(bsz * tt, width)
    bu = _dot(u.astype(BF16), bblk_ref[...])
    nt = nstate // LANE
    for c in range(2 * nt):
        hbuf[c] = bu[:, c * LANE:(c + 1) * LANE]
    ar = [jnp.broadcast_to(are_ref[:, c * LANE:(c + 1) * LANE], (bsz, LANE)) for c in range(nt)]
    ai = [jnp.broadcast_to(aim_ref[:, c * LANE:(c + 1) * LANE], (bsz, LANE)) for c in range(nt)]

    def step(t, carry):
        hr, hi = carry
        rows = pl.ds(t, bsz, stride=tt)
        nr, ni = [], []
        for c in range(nt):
            r = ar[c] * hr[c] - ai[c] * hi[c] + hbuf[c, rows, :]
            i = ar[c] * hi[c] + ai[c] * hr[c] + hbuf[nt + c, rows, :]
            hbuf[c, rows, :] = r
            hbuf[nt + c, rows, :] = i
            nr.append(r)
            ni.append(i)
        return tuple(nr), tuple(ni)

    init = (tuple(st_re[c] for c in range(nt)), tuple(st_im[c] for c in range(nt)))
    hr, hi = lax.fori_loop(0, tt, step, init)
    for c in range(nt):
        st_re[c] = hr[c]
        st_im[c] = hi[c]
    hs = jnp.concatenate([hbuf[c].astype(BF16) for c in range(2 * nt)], axis=-1)
    y = _dot(hs, cblk_ref[...]) + d_ref[...] * u
    z = _dot(_gelu_exact(y).astype(BF16), wglu_ref[...])
    dm = z.shape[-1] // 2
    s_out = z[:, :dm] * jax.nn.sigmoid(z[:, dm:])
    out_ref[...] = s_out.reshape(bsz, tt, dm).astype(BF16)


def _s5_scan(u3, ab_re, ab_im, bb_re, bb_im, c_re, c_im, d_skip, w_glu, tt):
    bsz, seq, width = u3.shape
    g, p, h = bb_re.shape
    nstate = g * p
    eye = jnp.eye(g, dtype=F32)
    blk_b = lambda bb: jnp.einsum('gph,gk->ghkp', bb, eye).reshape(g * h, nstate)
    bblk = jnp.concatenate([blk_b(bb_re), blk_b(bb_im)], axis=1).astype(BF16)
    blk_c = lambda c: jnp.einsum('ghp,gk->gpkh', c, eye).reshape(nstate, g * h)
    cblk = jnp.concatenate([blk_c(c_re), -blk_c(c_im)], axis=0).astype(BF16)
    dm2 = w_glu.shape[-1]
    consts = [bblk, ab_re.reshape(1, nstate), ab_im.reshape(1, nstate), cblk,
              d_skip.reshape(1, width), w_glu.astype(BF16)]
    return pl.pallas_call(
        functools.partial(_s5_scan_kernel, bsz=bsz, tt=tt, nstate=nstate),
        grid=(seq // tt,),
        in_specs=[pl.BlockSpec((bsz, tt, width), lambda i: (0, i, 0))] + [_const_spec(c.shape) for c in consts],
        out_specs=pl.BlockSpec((bsz, tt, dm2 // 2), lambda i: (0, i, 0)),
        out_shape=jax.ShapeDtypeStruct((bsz, seq, dm2 // 2), BF16),
        scratch_shapes=[pltpu.VMEM((2 * nstate // LANE, bsz * tt, LANE), F32),
                        pltpu.VMEM((nstate // LANE, bsz, LANE), F32),
                        pltpu.VMEM((nstate // LANE, bsz, LANE), F32)],
        compiler_params=_params("arbitrary"),
        name="s5_scan",
    )(u3, *consts)


def _attn_kernel(q_ref, k_ref, v_ref, o_ref, *, tq, tk):
    qi = pl.program_id(2)
    nfull = qi * (tq // tk)
    ndiag = tq // tk
    outs = []
    for hh in range(2):
        lanes = slice(hh * LANE, (hh + 1) * LANE)
        q = q_ref[:, lanes]

        def block(j, carry, masked):
            m, l, acc = carry
            rows = pl.ds(pl.multiple_of(j * tk, tk), tk)
            s = _dot_nt(q, k_ref[rows, lanes])
            if masked:
                qpos = qi * tq + lax.broadcasted_iota(jnp.int32, (tq, tk), 0)
                kpos = j * tk + lax.broadcasted_iota(jnp.int32, (tq, tk), 1)
                s = jnp.where(kpos <= qpos, s, NEG_INF)
            m_new = jnp.maximum(m, jnp.max(s, axis=-1, keepdims=True))
            alpha = jnp.exp(m - m_new)
            p = jnp.exp(s - m_new)
            l = alpha * l + jnp.sum(p, axis=-1, keepdims=True)
            acc = alpha * acc + _dot(p.astype(BF16), v_ref[rows, :])
            return m_new, l, acc

        init = (jnp.full((tq, 1), NEG_INF, F32), jnp.zeros((tq, 1), F32), jnp.zeros((tq, LANE), F32))
        carry = lax.fori_loop(0, nfull, functools.partial(block, masked=False), init)
        for dj in range(ndiag):
            carry = block(nfull + dj, carry, True)
        _, l, acc = carry
        outs.append(acc / l)
    lane = lax.broadcasted_iota(jnp.int32, (tq, LANE), 1)
    o_ref[...] = jnp.where(lane < MLA_V_DIM, outs[0], outs[1]).astype(BF16)


def _attention(q, k, v, tq, tk):
    bsz, seq, _ = q.shape
    return pl.pallas_call(
        functools.partial(_attn_kernel, tq=tq, tk=tk),
        grid=(bsz, MLA_HEADS // 2, seq // tq),
        in_specs=[pl.BlockSpec((None, tq, 2 * LANE), lambda b, h, i: (b, i, h)),
                  pl.BlockSpec((None, seq, 2 * LANE), lambda b, h, i: (b, 0, h)),
                  pl.BlockSpec((None, seq, 2 * MLA_V_DIM), lambda b, h, i: (b, 0, h))],
        out_specs=pl.BlockSpec((None, tq, 2 * MLA_V_DIM), lambda b, h, i: (b, i, h)),
        out_shape=jax.ShapeDtypeStruct((bsz, seq, MLA_HEADS * MLA_V_DIM), BF16),
        compiler_params=_params("parallel", "parallel", "arbitrary"),
        name="mla_attention",
    )(q, k, v)


def _mixer_out_kernel(x_ref, ao_ref, so_ref, gate_ref, kx_ref, vx_ref, lng_ref, lnb_ref, w_oa_ref, w_o_ref,
                      ln1g_ref, ln1b_ref, w_xq_ref, w_xo_ref, ln2g_ref, ln2b_ref, out_ref, *, x_scale):
    h0 = _layer_norm(x_ref[...], lng_ref[...], lnb_ref[...])
    d = h0.shape[-1]
    a_out = _dot(ao_ref[...], w_oa_ref[...])
    gate = gate_ref[...]
    mixed = gate[:, :d].astype(F32) * so_ref[...].astype(F32) + gate[:, d:].astype(F32) * a_out
    mix = _dot(mixed.astype(BF16), w_o_ref[...])
    h1 = _layer_norm(DN_ALPHA * h0 + mix, ln1g_ref[...], ln1b_ref[...])
    qx = (_dot(h1.astype(BF16), w_xq_ref[...]) * x_scale).astype(BF16)
    hd = d // XATTN_HEADS
    xa = jnp.zeros_like(h1)
    for hh in range(XATTN_HEADS):
        cols = slice(hh * hd, (hh + 1) * hd)
        s = _dot_nt(qx[:, cols], kx_ref[:, cols])
        p = jnp.exp(s - jnp.max(s, axis=-1, keepdims=True))
        o = _dot(p.astype(BF16), vx_ref[:, cols]) / jnp.sum(p, axis=-1, keepdims=True)
        xa = xa + _dot(o.astype(BF16), w_xo_ref[cols, :])
    out_ref[...] = _layer_norm(DN_ALPHA * h1 + xa, ln2g_ref[...], ln2b_ref[...])


def _mixer_out(x, ao, so, gate, kx, vx, ln_g, ln_b, w_oa, w_o, ln1_g, ln1_b, w_xq, w_xo, ln2_g, ln2_b, tm):
    bsz, seq, d = x.shape
    m = kx.shape[1]
    vec = lambda a: a.reshape(1, d)
    tok = lambda w: pl.BlockSpec((None, tm, w), lambda b, i: (b, i, 0))
    per_b = pl.BlockSpec((None, m, d), lambda b, i: (b, 0, 0))
    consts = [vec(ln_g), vec(ln_b), w_oa.astype(BF16), w_o.astype(BF16), vec(ln1_g), vec(ln1_b),
              w_xq.astype(BF16), w_xo.astype(BF16), vec(ln2_g), vec(ln2_b)]
    return pl.pallas_call(
        functools.partial(_mixer_out_kernel, x_scale=(d // XATTN_HEADS) ** -0.5),
        grid=(bsz, seq // tm),
        in_specs=[tok(d), tok(ao.shape[-1]), tok(d), tok(2 * d), per_b, per_b] + [_const_spec(c.shape) for c in consts],
        out_specs=tok(d),
        out_shape=jax.ShapeDtypeStruct((bsz, seq, d), F32),
        compiler_params=_params("parallel", "parallel"),
        name="mixer_out",
    )(x, ao, so, gate, kx, vx, *consts)


def _mlp_kernel(h_ref, w_up_ref, w_down_ref, g_ref, b_ref, out_ref, *, chunk):
    h = h_ref[...]
    hb = h.astype(BF16)
    hidden = w_up_ref.shape[-1]
    ff = jnp.zeros_like(h)
    for c in range(hidden // chunk):
        cols = slice(c * chunk, (c + 1) * chunk)
        a = jnp.maximum(_dot(hb, w_up_ref[:, cols]), 0.0)
        ff = ff + _dot((a * a).astype(BF16), w_down_ref[cols, :])
    out_ref[...] = _layer_norm(DN_ALPHA * h + ff, g_ref[...], b_ref[...])


def _mlp(h2, w_up, w_down, ln_g, ln_b, tm):
    n, d = h2.shape
    hidden = w_up.shape[-1]
    consts = [w_up.astype(BF16), w_down.astype(BF16), ln_g.reshape(1, d), ln_b.reshape(1, d)]
    return pl.pallas_call(
        functools.partial(_mlp_kernel, chunk=min(hidden, 1024)),
        grid=(n // tm,),
        in_specs=[pl.BlockSpec((tm, d), lambda i: (i, 0))] + [_const_spec(c.shape) for c in consts],
        out_specs=pl.BlockSpec((tm, d), lambda i: (i, 0)),
        out_shape=jax.ShapeDtypeStruct((n, d), F32),
        compiler_params=_params("parallel"),
        name="mlp",
    )(h2, *consts)


def kernel(x, mem, positions, ln_in_g, ln_in_b, w_in, s5_lam_re, s5_lam_im, s5_log_dt, s5_b_re, s5_b_im,
           s5_c_re, s5_c_im, s5_d, w_glu, q_norm_g, w_uq, kv_norm_g, w_ukv, w_oa, w_o, ln1_g, ln1_b,
           w_xq, w_xk, w_xv, w_xo, ln2_g, ln2_b, w_up, w_down, ln3_g, ln3_b):
    bsz, seq, d = x.shape
    n = bsz * seq
    assert w_in.shape[0] == DEPTH == 1
    tm = min(seq, 512)
    tq = tk = min(seq, 512)
    tt = min(seq, 128)

    cos, sin = _rope_tables(positions)
    ones = jnp.ones((bsz, seq, MLA_NOPE_DIM), F32)
    zeros = jnp.zeros((bsz, seq, LANE - MLA_QK_DIM), F32)
    cos128 = jnp.concatenate([ones, cos, cos, zeros], axis=-1).reshape(n, LANE)
    sin128 = jnp.concatenate([0.0 * ones, sin, sin, zeros], axis=-1).reshape(n, LANE)

    u, q, k, v, gate = _in_proj(x.reshape(n, d), cos128, sin128, ln_in_g, ln_in_b, w_in[0],
                                q_norm_g[0], w_uq[0], kv_norm_g[0], w_ukv[0], tm)

    ab_re, ab_im, bb_re, bb_im = _s5_discretize(s5_lam_re[0], s5_lam_im[0], s5_log_dt[0], s5_b_re[0], s5_b_im[0])
    s_out = _s5_scan(u.reshape(bsz, seq, -1), ab_re, ab_im, bb_re, bb_im, s5_c_re[0], s5_c_im[0],
                     s5_d[0], w_glu[0], tt)

    hl = MLA_HEADS * LANE
    a_o = _attention(q.reshape(bsz, seq, hl), k.reshape(bsz, seq, hl), v.reshape(bsz, seq, -1), tq, tk)

    kx, vx = _mem_kv(mem, w_xk[0], w_xv[0])
    h2 = _mixer_out(x, a_o, s_out, gate.reshape(bsz, seq, 2 * d), kx, vx, ln_in_g, ln_in_b, w_oa[0], w_o[0],
                    ln1_g[0], ln1_b[0], w_xq[0], w_xo[0], ln2_g[0], ln2_b[0], tm)
    out = _mlp(h2.reshape(n, d), w_up[0], w_down[0], ln3_g[0], ln3_b[0], tm)
    return out.reshape(bsz, seq, d)
```

```python
import functools
import math

import jax
import jax.numpy as jnp
from jax import lax
from jax.experimental import pallas as pl
from jax.experimental.pallas import tpu as pltpu

F32 = jnp.float32
BF16 = jnp.bfloat16

S5_GROUP_CH = 16
S5_STATE = 64
S5_MAX_RE = -1e-4
MLA_HEADS = 8
MLA_NOPE_DIM = 64
MLA_ROPE_DIM = 32
MLA_QK_DIM = MLA_NOPE_DIM + MLA_ROPE_DIM
MLA_V_DIM = 64
ROPE_THETA = 10000.0
XATTN_HEADS = 4
LN_EPS = 1e-5
RMS_EPS = 1e-6
NEG_INF = -1e30
DEPTH = 1
DN_ALPHA = (2.0 * DEPTH) ** 0.25

LANE = 128
VMEM_LIMIT = 56 * 1024 * 1024


def _const_spec(shape):
    nd = len(shape)
    return pl.BlockSpec(shape, lambda *_: (0,) * nd, pipeline_mode=pl.Buffered(1))


def _params(*sem):
    return pltpu.CompilerParams(dimension_semantics=sem, vmem_limit_bytes=VMEM_LIMIT)


def _layer_norm(x, g, b):
    mu = jnp.mean(x, axis=-1, keepdims=True)
    xc = x - mu
    var = jnp.mean(xc * xc, axis=-1, keepdims=True)
    return xc * lax.rsqrt(var + LN_EPS) * g + b


def _rms_norm(x, g):
    return x * lax.rsqrt(jnp.mean(x * x, axis=-1, keepdims=True) + RMS_EPS) * g


def _gelu_exact(x):
    return 0.5 * x * (1.0 + lax.erf(x * (0.5 ** 0.5)))


def _dot(a, b):
    return jnp.dot(a, b, preferred_element_type=F32)


def _dot_nt(a, b):
    return lax.dot_general(a, b, (((1,), (1,)), ((), ())), preferred_element_type=F32)


def _s5_disc_kernel(lr_ref, li_ref, ldt_ref, br_ref, bi_ref, are_ref, aim_ref, bbr_ref, bbi_ref):
    lr = jnp.minimum(lr_ref[...], S5_MAX_RE)
    li = li_ref[...]
    dt = jnp.exp(ldt_ref[...])
    mag = jnp.exp(lr * dt)
    ang = li * dt
    ab_re = mag * jnp.cos(ang)
    ab_im = mag * jnp.sin(ang)
    den = lr * lr + li * li
    nr = ab_re - 1.0
    f_re = (nr * lr + ab_im * li) / den
    f_im = (ab_im * lr - nr * li) / den
    br = br_ref[...]
    bi = bi_ref[...]
    are_ref[...] = ab_re
    aim_ref[...] = ab_im
    bbr_ref[...] = f_re * br - f_im * bi
    bbi_ref[...] = f_re * bi + f_im * br


def _s5_discretize(lam_re, lam_im, log_dt, b_re, b_im):
    g, p = lam_re.shape
    h = b_re.shape[-1]
    n = g * p
    col = lambda a: a.reshape(n, 1)
    ldt = jnp.broadcast_to(log_dt[:, None], (g, p))
    outs = pl.pallas_call(
        _s5_disc_kernel,
        out_shape=[jax.ShapeDtypeStruct((n, 1), F32), jax.ShapeDtypeStruct((n, 1), F32),
                   jax.ShapeDtypeStruct((n, h), F32), jax.ShapeDtypeStruct((n, h), F32)],
        name="s5_discretize",
    )(col(lam_re), col(lam_im), col(ldt), b_re.reshape(n, h), b_im.reshape(n, h))
    ab_re, ab_im, bb_re, bb_im = outs
    return ab_re.reshape(g, p), ab_im.reshape(g, p), bb_re.reshape(g, p, h), bb_im.reshape(g, p, h)


def _rope_kernel(pos_ref, inv_ref, cos_ref, sin_ref):
    ang = pos_ref[...].astype(F32) * inv_ref[...]
    cos_ref[...] = jnp.cos(ang)
    sin_ref[...] = jnp.sin(ang)


def _rope_tables(positions):
    bsz, seq = positions.shape
    half = MLA_ROPE_DIM // 2
    per_row = LANE // half
    rows = bsz * seq // per_row
    inv = ROPE_THETA ** (-jnp.arange(0, MLA_ROPE_DIM, 2, dtype=F32) / MLA_ROPE_DIM)
    inv_row = jnp.tile(inv, per_row).reshape(1, LANE)
    pos_rep = jnp.repeat(positions.reshape(-1), half).reshape(rows, LANE)
    tr = min(rows, 1024)
    cos, sin = pl.pallas_call(
        _rope_kernel,
        grid=(rows // tr,),
        in_specs=[pl.BlockSpec((tr, LANE), lambda i: (i, 0)), _const_spec((1, LANE))],
        out_specs=[pl.BlockSpec((tr, LANE), lambda i: (i, 0))] * 2,
        out_shape=[jax.ShapeDtypeStruct((rows, LANE), F32)] * 2,
        compiler_params=_params("parallel"),
        name="rope_table",
    )(pos_rep, inv_row)
    return cos.reshape(bsz, seq, half), sin.reshape(bsz, seq, half)


def _mem_kv_kernel(mem_ref, wk_ref, wv_ref, k_ref, v_ref):
    m = mem_ref[...].astype(BF16)
    k_ref[...] = _dot(m, wk_ref[...]).astype(BF16)
    v_ref[...] = _dot(m, wv_ref[...]).astype(BF16)


def _mem_kv(mem, w_xk, w_xv):
    bsz, m, d = mem.shape
    return pl.pallas_call(
        _mem_kv_kernel,
        grid=(bsz,),
        in_specs=[pl.BlockSpec((None, m, d), lambda b: (b, 0, 0)), _const_spec((d, d)), _const_spec((d, d))],
        out_specs=[pl.BlockSpec((None, m, d), lambda b: (b, 0, 0))] * 2,
        out_shape=[jax.ShapeDtypeStruct((bsz, m, d), BF16)] * 2,
        compiler_params=_params("parallel"),
        name="mem_kv",
    )(mem, w_xk.astype(BF16), w_xv.astype(BF16))


def _in_proj_kernel(x_ref, cos_ref, sin_ref, lng_ref, lnb_ref, w_u_ref, w_cq_ref, w_ckv_ref, w_kr_ref,
                    w_g_ref, qg_ref, kvg_ref, w_q_ref, w_qs_ref, w_k_ref, w_v_ref,
                    u_ref, q_ref, k_ref, v_ref, gate_ref, *, q_scale):
    hb = _layer_norm(x_ref[...], lng_ref[...], lnb_ref[...]).astype(BF16)
    cos = cos_ref[...]
    sin = sin_ref[...]
    u_ref[...] = _dot(hb, w_u_ref[...])
    gate_ref[...] = jax.nn.sigmoid(_dot(hb, w_g_ref[...])).astype(BF16)
    kr2 = _dot(hb, w_kr_ref[...])
    k_rope = kr2[:, :LANE] * cos + kr2[:, LANE:] * sin
    cqn = _rms_norm(_dot(hb, w_cq_ref[...]), qg_ref[...]).astype(BF16)
    qa = _dot(cqn, w_q_ref[...])
    qb = _dot(cqn, w_qs_ref[...])
    ckvn = _rms_norm(_dot(hb, w_ckv_ref[...]), kvg_ref[...]).astype(BF16)
    kn = _dot(ckvn, w_k_ref[...])
    v_ref[...] = _dot(ckvn, w_v_ref[...]).astype(BF16)
    for h in range(MLA_HEADS):
        sl = slice(h * LANE, (h + 1) * LANE)
        q_ref[:, sl] = ((qa[:, sl] * cos + qb[:, sl] * sin) * q_scale).astype(BF16)
        k_ref[:, sl] = (kn[:, sl] + k_rope).astype(BF16)


def _rotate_half_cols(w):
    half = MLA_ROPE_DIM // 2
    return jnp.concatenate([-w[..., half:], w[..., :half]], axis=-1)


def _head_slab(nope, rope):
    d, h, _ = nope.shape
    pad = jnp.zeros((d, h, LANE - MLA_QK_DIM), nope.dtype)
    return jnp.concatenate([nope, rope, pad], axis=-1).reshape(d, h * LANE)


def _in_proj(x2, cos128, sin128, ln_g, ln_b, w_in, q_norm_g, w_uq, kv_norm_g, w_ukv, tm):
    n, d = x2.shape
    s5w = q_rank = kv_rank = d // 4
    o1, o2, o3 = s5w, s5w + q_rank, s5w + q_rank + kv_rank
    o4 = o3 + MLA_ROPE_DIM
    hl = MLA_HEADS * LANE
    w_u = w_in[:, :o1].astype(BF16)
    w_cq = w_in[:, o1:o2].astype(BF16)
    w_ckv = w_in[:, o2:o3].astype(BF16)
    w_kr = w_in[:, o3:o4]
    zl = jnp.zeros((d, MLA_NOPE_DIM), F32)
    zr = jnp.zeros((d, LANE - MLA_QK_DIM), F32)
    w_kr2 = jnp.concatenate([zl, w_kr, zr, zl, _rotate_half_cols(w_kr), zr], axis=-1).astype(BF16)
    w_g = w_in[:, o4:].astype(BF16)
    wq3 = w_uq.reshape(q_rank, MLA_HEADS, MLA_QK_DIM)
    wq_nope, wq_rope = wq3[..., :MLA_NOPE_DIM], wq3[..., MLA_NOPE_DIM:]
    w_q = _head_slab(wq_nope, wq_rope).astype(BF16)
    w_qs = _head_slab(jnp.zeros_like(wq_nope), _rotate_half_cols(wq_rope)).astype(BF16)
    wkv3 = w_ukv.reshape(kv_rank, MLA_HEADS, MLA_NOPE_DIM + MLA_V_DIM)
    w_k = _head_slab(wkv3[..., :MLA_NOPE_DIM],
                     jnp.zeros((kv_rank, MLA_HEADS, MLA_ROPE_DIM), F32)).astype(BF16)
    w_v = wkv3[..., MLA_NOPE_DIM:].reshape(kv_rank, MLA_HEADS * MLA_V_DIM).astype(BF16)
    row = lambda w: pl.BlockSpec((tm, w), lambda i: (i, 0))
    consts = [ln_g.reshape(1, d), ln_b.reshape(1, d), w_u, w_cq, w_ckv, w_kr2, w_g,
              q_norm_g.reshape(1, q_rank), kv_norm_g.reshape(1, kv_rank), w_q, w_qs, w_k, w_v]
    return pl.pallas_call(
        functools.partial(_in_proj_kernel, q_scale=MLA_QK_DIM ** -0.5 * math.log2(math.e)),
        grid=(n // tm,),
        in_specs=[row(d), row(LANE), row(LANE)] + [_const_spec(c.shape) for c in consts],
        out_specs=[row(s5w), row(hl), row(hl), row(MLA_HEADS * MLA_V_DIM), row(2 * d)],
        out_shape=[jax.ShapeDtypeStruct((n, s5w), F32), jax.ShapeDtypeStruct((n, hl), BF16),
                   jax.ShapeDtypeStruct((n, hl), BF16),
                   jax.ShapeDtypeStruct((n, MLA_HEADS * MLA_V_DIM), BF16),
                   jax.ShapeDtypeStruct((n, 2 * d), BF16)],
        compiler_params=_params("parallel"),
        name="in_proj",
    )(x2, cos128, sin128, *consts)


def _s5_scan_kernel(u_ref, bblk_ref, are_ref, aim_ref, cblk_ref, d_ref, wglu_ref, out_ref,
                    hbuf, st_re, st_im, *, bsz, tt, nstate):
    @pl.when(pl.program_id(0) == 0)
    def _():
        st_re[...] = jnp.zeros_like(st_re)
        st_im[...] = jnp.zeros_like(st_im)

    width = u_ref.shape[-1]
    u = u_ref[...].reshape(bsz * tt, width)
    bu = _dot(u.astype(BF16), bblk_ref[...])
    nt = nstate // LANE
    for c in range(2 * nt):
        hbuf[c] = bu[:, c * LANE:(c + 1) * LANE]
    ar = [jnp.broadcast_to(are_ref[:, c * LANE:(c + 1) * LANE], (bsz, LANE)) for c in range(nt)]
    ai = [jnp.broadcast_to(aim_ref[:, c * LANE:(c + 1) * LANE], (bsz, LANE)) for c in range(nt)]

    def step(t, carry):
        hr, hi = carry
        rows = pl.ds(t, bsz, stride=tt)
        nr, ni = [], []
        for c in range(nt):
            r = ar[c] * hr[c] - ai[c] * hi[c] + hbuf[c, rows, :]
            i = ar[c] * hi[c] + ai[c] * hr[c] + hbuf[nt + c, rows, :]
            hbuf[c, rows, :] = r
            hbuf[nt + c, rows, :] = i
            nr.append(r)
            ni.append(i)
        return tuple(nr), tuple(ni)

    init = (tuple(st_re[c] for c in range(nt)), tuple(st_im[c] for c in range(nt)))
    hr, hi = lax.fori_loop(0, tt, step, init)
    for c in range(nt):
        st_re[c] = hr[c]
        st_im[c] = hi[c]
    hs = jnp.concatenate([hbuf[c].astype(BF16) for c in range(2 * nt)], axis=-1)
    y = _dot(hs, cblk_ref[...]) + d_ref[...] * u
    z = _dot(_gelu_exact(y).astype(BF16), wglu_ref[...])
    dm = z.shape[-1] // 2
    s_out = z[:, :dm] * jax.nn.sigmoid(z[:, dm:])
    out_ref[...] = s_out.reshape(bsz, tt, dm).astype(BF16)


def _s5_scan(u3, ab_re, ab_im, bb_re, bb_im, c_re, c_im, d_skip, w_glu, tt):
    bsz, seq, width = u3.shape
    g, p, h = bb_re.shape
    nstate = g * p
    eye = jnp.eye(g, dtype=F32)
    blk_b = lambda bb: jnp.einsum('gph,gk->ghkp', bb, eye).reshape(g * h, nstate)
    bblk = jnp.concatenate([blk_b(bb_re), blk_b(bb_im)], axis=1).astype(BF16)
    blk_c = lambda c: jnp.einsum('ghp,gk->gpkh', c, eye).reshape(nstate, g * h)
    cblk = jnp.concatenate([blk_c(c_re), -blk_c(c_im)], axis=0).astype(BF16)
    dm2 = w_glu.shape[-1]
    consts = [bblk, ab_re.reshape(1, nstate), ab_im.reshape(1, nstate), cblk,
              d_skip.reshape(1, width), w_glu.astype(BF16)]
    return pl.pallas_call(
        functools.partial(_s5_scan_kernel, bsz=bsz, tt=tt, nstate=nstate),
        grid=(seq // tt,),
        in_specs=[pl.BlockSpec((bsz, tt, width), lambda i: (0, i, 0))] + [_const_spec(c.shape) for c in consts],
        out_specs=pl.BlockSpec((bsz, tt, dm2 // 2), lambda i: (0, i, 0)),
        out_shape=jax.ShapeDtypeStruct((bsz, seq, dm2 // 2), BF16),
        scratch_shapes=[pltpu.VMEM((2 * nstate // LANE, bsz * tt, LANE), F32),
                        pltpu.VMEM((nstate // LANE, bsz, LANE), F32),
                        pltpu.VMEM((nstate // LANE, bsz, LANE), F32)],
        compiler_params=_params("arbitrary"),
        name="s5_scan",
    )(u3, *consts)


def _attn_kernel(q_ref, k_ref, vt_ref, o_ref, *, tq, nh):
    qi = pl.program_id(2)
    heads = [slice(h * LANE, (h + 1) * LANE) for h in range(nh)]
    qs = [q_ref[:, lanes] for lanes in heads]

    def block(j, carry, masked):
        rows = pl.ds(pl.multiple_of(j * tq, tq), tq)
        sts = [_dot_nt(k_ref[rows, lanes], q) for lanes, q in zip(heads, qs)]
        if masked:
            keep = (lax.broadcasted_iota(jnp.int32, (tq, tq), 0) <= lax.broadcasted_iota(jnp.int32, (tq, tq), 1))
        out = []
        for h, ((m, l, acc), st) in enumerate(zip(carry, sts)):
            if masked:
                st = jnp.where(keep, st, NEG_INF)
            m_new = jnp.maximum(m, jnp.max(st, axis=0, keepdims=True))
            alpha = jnp.exp2(m - m_new)
            pt = jnp.exp2(st - m_new)
            l = alpha * l + jnp.sum(pt, axis=0, keepdims=True)
            vt = vt_ref[j, h * MLA_V_DIM:(h + 1) * MLA_V_DIM, :]
            acc = alpha * acc + _dot(vt, pt.astype(BF16))
            out.append((m_new, l, acc))
        return tuple(out)

    one = (jnp.full((1, tq), NEG_INF, F32), jnp.zeros((1, tq), F32), jnp.zeros((MLA_V_DIM, tq), F32))
    carry = lax.fori_loop(0, qi, lambda j, c: block(j, c, False), (one,) * nh)
    carry = block(qi, carry, True)
    ot = jnp.concatenate([acc / l for _, l, acc in carry], axis=0)
    o_ref[...] = ot.T.astype(BF16)


def _attention(q, k, v, tq, nh):
    bsz, seq, _ = q.shape
    groups = MLA_HEADS // nh
    vt = v.reshape(bsz, seq // tq, tq, groups, nh * MLA_V_DIM).transpose(0, 3, 1, 4, 2)
    return pl.pallas_call(
        functools.partial(_attn_kernel, tq=tq, nh=nh),
        grid=(bsz, groups, seq // tq),
        in_specs=[pl.BlockSpec((None, tq, nh * LANE), lambda b, h, i: (b, i, h)),
                  pl.BlockSpec((None, seq, nh * LANE), lambda b, h, i: (b, 0, h)),
                  pl.BlockSpec((None, None, seq // tq, nh * MLA_V_DIM, tq), lambda b, h, i: (b, h, 0, 0, 0))],
        out_specs=pl.BlockSpec((None, tq, nh * MLA_V_DIM), lambda b, h, i: (b, i, h)),
        out_shape=jax.ShapeDtypeStruct((bsz, seq, MLA_HEADS * MLA_V_DIM), BF16),
        compiler_params=_params("parallel", "parallel", "arbitrary"),
        name="mla_attention",
    )(q, k, vt)


def _mixer_out_kernel(x_ref, ao_ref, so_ref, gate_ref, kx_ref, vx_ref, lng_ref, lnb_ref, w_oa_ref, w_o_ref,
                      ln1g_ref, ln1b_ref, w_xq_ref, w_xo_ref, ln2g_ref, ln2b_ref, out_ref, *, x_scale):
    h0 = _layer_norm(x_ref[...], lng_ref[...], lnb_ref[...])
    d = h0.shape[-1]
    a_out = _dot(ao_ref[...], w_oa_ref[...])
    gate = gate_ref[...]
    mixed = gate[:, :d].astype(F32) * so_ref[...].astype(F32) + gate[:, d:].astype(F32) * a_out
    mix = _dot(mixed.astype(BF16), w_o_ref[...])
    h1 = _layer_norm(DN_ALPHA * h0 + mix, ln1g_ref[...], ln1b_ref[...])
    qx = (_dot(h1.astype(BF16), w_xq_ref[...]) * x_scale).astype(BF16)
    hd = d // XATTN_HEADS
    xa = jnp.zeros_like(h1)
    for hh in range(XATTN_HEADS):
        cols = slice(hh * hd, (hh + 1) * hd)
        s = _dot_nt(qx[:, cols], kx_ref[:, cols])
        p = jnp.exp(s - jnp.max(s, axis=-1, keepdims=True))
        o = _dot(p.astype(BF16), vx_ref[:, cols]) / jnp.sum(p, axis=-1, keepdims=True)
        xa = xa + _dot(o.astype(BF16), w_xo_ref[cols, :])
    out_ref[...] = _layer_norm(DN_ALPHA * h1 + xa, ln2g_ref[...], ln2b_ref[...])


def _mixer_out(x, ao, so, gate, kx, vx, ln_g, ln_b, w_oa, w_o, ln1_g, ln1_b, w_xq, w_xo, ln2_g, ln2_b, tm):
    bsz, seq, d = x.shape
    m = kx.shape[1]
    vec = lambda a: a.reshape(1, d)
    tok = lambda w: pl.BlockSpec((None, tm, w), lambda b, i: (b, i, 0))
    per_b = pl.BlockSpec((None, m, d), lambda b, i: (b, 0, 0))
    consts = [vec(ln_g), vec(ln_b), w_oa.astype(BF16), w_o.astype(BF16), vec(ln1_g), vec(ln1_b),
              w_xq.astype(BF16), w_xo.astype(BF16), vec(ln2_g), vec(ln2_b)]
    return pl.pallas_call(
        functools.partial(_mixer_out_kernel, x_scale=(d // XATTN_HEADS) ** -0.5),
        grid=(bsz, seq // tm),
        in_specs=[tok(d), tok(ao.shape[-1]), tok(d), tok(2 * d), per_b, per_b] + [_const_spec(c.shape) for c in consts],
        out_specs=tok(d),
        out_shape=jax.ShapeDtypeStruct((bsz, seq, d), F32),
        compiler_params=_params("parallel", "parallel"),
        name="mixer_out",
    )(x, ao, so, gate, kx, vx, *consts)


def _mlp_kernel(h_ref, w_up_ref, w_down_ref, g_ref, b_ref, out_ref, *, chunk):
    h = h_ref[...]
    hb = h.astype(BF16)
    hidden = w_up_ref.shape[-1]
    ff = jnp.zeros_like(h)
    for c in range(hidden // chunk):
        cols = slice(c * chunk, (c + 1) * chunk)
        a = jnp.maximum(_dot(hb, w_up_ref[:, cols]), 0.0)
        ff = ff + _dot((a * a).astype(BF16), w_down_ref[cols, :])
    out_ref[...] = _layer_norm(DN_ALPHA * h + ff, g_ref[...], b_ref[...])


def _mlp(h2, w_up, w_down, ln_g, ln_b, tm):
    n, d = h2.shape
    hidden = w_up.shape[-1]
    consts = [w_up.astype(BF16), w_down.astype(BF16), ln_g.reshape(1, d), ln_b.reshape(1, d)]
    return pl.pallas_call(
        functools.partial(_mlp_kernel, chunk=min(hidden, 1024)),
        grid=(n // tm,),
        in_specs=[pl.BlockSpec((tm, d), lambda i: (i, 0))] + [_const_spec(c.shape) for c in consts],
        out_specs=pl.BlockSpec((tm, d), lambda i: (i, 0)),
        out_shape=jax.ShapeDtypeStruct((n, d), F32),
        compiler_params=_params("parallel"),
        name="mlp",
    )(h2, *consts)


def kernel(x, mem, positions, ln_in_g, ln_in_b, w_in, s5_lam_re, s5_lam_im, s5_log_dt, s5_b_re, s5_b_im,
           s5_c_re, s5_c_im, s5_d, w_glu, q_norm_g, w_uq, kv_norm_g, w_ukv, w_oa, w_o, ln1_g, ln1_b,
           w_xq, w_xk, w_xv, w_xo, ln2_g, ln2_b, w_up, w_down, ln3_g, ln3_b):
    bsz, seq, d = x.shape
    n = bsz * seq
    assert w_in.shape[0] == DEPTH == 1
    tm = min(seq, 512)
    tq = min(seq, 512)
    nh = 4
    tt = min(seq, 128)

    cos, sin = _rope_tables(positions)
    ones = jnp.ones((bsz, seq, MLA_NOPE_DIM), F32)
    zeros = jnp.zeros((bsz, seq, LANE - MLA_QK_DIM), F32)
    cos128 = jnp.concatenate([ones, cos, cos, zeros], axis=-1).reshape(n, LANE)
    sin128 = jnp.concatenate([0.0 * ones, sin, sin, zeros], axis=-1).reshape(n, LANE)

    u, q, k, v, gate = _in_proj(x.reshape(n, d), cos128, sin128, ln_in_g, ln_in_b, w_in[0],
                                q_norm_g[0], w_uq[0], kv_norm_g[0], w_ukv[0], tm)

    ab_re, ab_im, bb_re, bb_im = _s5_discretize(s5_lam_re[0], s5_lam_im[0], s5_log_dt[0], s5_b_re[0], s5_b_im[0])
    s_out = _s5_scan(u.reshape(bsz, seq, -1), ab_re, ab_im, bb_re, bb_im, s5_c_re[0], s5_c_im[0],
                     s5_d[0], w_glu[0], tt)

    hl = MLA_HEADS * LANE
    a_o = _attention(q.reshape(bsz, seq, hl), k.reshape(bsz, seq, hl), v.reshape(bsz, seq, -1), tq, nh)

    kx, vx = _mem_kv(mem, w_xk[0], w_xv[0])
    h2 = _mixer_out(x, a_o, s_out, gate.reshape(bsz, seq, 2 * d), kx, vx, ln_in_g, ln_in_b, w_oa[0], w_o[0],
                    ln1_g[0], ln1_b[0], w_xq[0], w_xo[0], ln2_g[0], ln2_b[0], tm)
    out = _mlp(h2.reshape(n, d), w_up[0], w_down[0], ln3_g[0], ln3_b[0], tm)
    return out.reshape(bsz, seq, d)
```

```python
import functools
import math

import jax
import jax.numpy as jnp
from jax import lax
from jax.experimental import pallas as pl
from jax.experimental.pallas import tpu as pltpu

F32 = jnp.float32
BF16 = jnp.bfloat16

S5_GROUP_CH = 16
S5_STATE = 64
S5_MAX_RE = -1e-4
MLA_HEADS = 8
MLA_NOPE_DIM = 64
MLA_ROPE_DIM = 32
MLA_QK_DIM = MLA_NOPE_DIM + MLA_ROPE_DIM
MLA_V_DIM = 64
ROPE_THETA = 10000.0
XATTN_HEADS = 4
LN_EPS = 1e-5
RMS_EPS = 1e-6
NEG_INF = -1e30
DEPTH = 1
DN_ALPHA = (2.0 * DEPTH) ** 0.25

SCAN_UNROLL = 4
LANE = 128
VMEM_LIMIT = 56 * 1024 * 1024


def _const_spec(shape):
    nd = len(shape)
    return pl.BlockSpec(shape, lambda *_: (0,) * nd, pipeline_mode=pl.Buffered(1))


def _params(*sem):
    return pltpu.CompilerParams(dimension_semantics=sem, vmem_limit_bytes=VMEM_LIMIT)


def _layer_norm(x, g, b):
    mu = jnp.mean(x, axis=-1, keepdims=True)
    xc = x - mu
    var = jnp.mean(xc * xc, axis=-1, keepdims=True)
    return xc * lax.rsqrt(var + LN_EPS) * g + b


def _rms_norm(x, g):
    return x * lax.rsqrt(jnp.mean(x * x, axis=-1, keepdims=True) + RMS_EPS) * g


def _gelu_exact(x):
    return 0.5 * x * (1.0 + lax.erf(x * (0.5 ** 0.5)))


def _dot(a, b):
    return jnp.dot(a, b, preferred_element_type=F32)


def _dot_nt(a, b):
    return lax.dot_general(a, b, (((1,), (1,)), ((), ())), preferred_element_type=F32)


def _s5_disc_kernel(lr_ref, li_ref, ldt_ref, br_ref, bi_ref, are_ref, aim_ref, bbr_ref, bbi_ref):
    lr = jnp.minimum(lr_ref[...], S5_MAX_RE)
    li = li_ref[...]
    dt = jnp.exp(ldt_ref[...])
    mag = jnp.exp(lr * dt)
    ang = li * dt
    ab_re = mag * jnp.cos(ang)
    ab_im = mag * jnp.sin(ang)
    den = lr * lr + li * li
    nr = ab_re - 1.0
    f_re = (nr * lr + ab_im * li) / den
    f_im = (ab_im * lr - nr * li) / den
    br = br_ref[...]
    bi = bi_ref[...]
    are_ref[...] = ab_re
    aim_ref[...] = ab_im
    bbr_ref[...] = f_re * br - f_im * bi
    bbi_ref[...] = f_re * bi + f_im * br


def _s5_discretize(lam_re, lam_im, log_dt, b_re, b_im):
    g, p = lam_re.shape
    h = b_re.shape[-1]
    n = g * p
    col = lambda a: a.reshape(n, 1)
    ldt = jnp.broadcast_to(log_dt[:, None], (g, p))
    outs = pl.pallas_call(
        _s5_disc_kernel,
        out_shape=[jax.ShapeDtypeStruct((n, 1), F32), jax.ShapeDtypeStruct((n, 1), F32),
                   jax.ShapeDtypeStruct((n, h), F32), jax.ShapeDtypeStruct((n, h), F32)],
        name="s5_discretize",
    )(col(lam_re), col(lam_im), col(ldt), b_re.reshape(n, h), b_im.reshape(n, h))
    ab_re, ab_im, bb_re, bb_im = outs
    return ab_re.reshape(g, p), ab_im.reshape(g, p), bb_re.reshape(g, p, h), bb_im.reshape(g, p, h)


def _rope_kernel(pos_ref, inv_ref, cos_ref, sin_ref):
    ang = pos_ref[...].astype(F32) * inv_ref[...]
    cos_ref[...] = jnp.cos(ang)
    sin_ref[...] = jnp.sin(ang)


def _rope_tables(positions):
    bsz, seq = positions.shape
    half = MLA_ROPE_DIM // 2
    per_row = LANE // half
    rows = bsz * seq // per_row
    inv = ROPE_THETA ** (-jnp.arange(0, MLA_ROPE_DIM, 2, dtype=F32) / MLA_ROPE_DIM)
    inv_row = jnp.tile(inv, per_row).reshape(1, LANE)
    pos_rep = jnp.repeat(positions.reshape(-1), half).reshape(rows, LANE)
    tr = min(rows, 1024)
    cos, sin = pl.pallas_call(
        _rope_kernel,
        grid=(rows // tr,),
        in_specs=[pl.BlockSpec((tr, LANE), lambda i: (i, 0)), _const_spec((1, LANE))],
        out_specs=[pl.BlockSpec((tr, LANE), lambda i: (i, 0))] * 2,
        out_shape=[jax.ShapeDtypeStruct((rows, LANE), F32)] * 2,
        compiler_params=_params("parallel"),
        name="rope_table",
    )(pos_rep, inv_row)
    return cos.reshape(bsz, seq, half), sin.reshape(bsz, seq, half)


def _mem_kv_kernel(mem_ref, wk_ref, wv_ref, k_ref, v_ref):
    m = mem_ref[...].astype(BF16)
    k_ref[...] = _dot(m, wk_ref[...]).astype(BF16)
    v_ref[...] = _dot(m, wv_ref[...]).astype(BF16)


def _mem_kv(mem, w_xk, w_xv):
    bsz, m, d = mem.shape
    return pl.pallas_call(
        _mem_kv_kernel,
        grid=(bsz,),
        in_specs=[pl.BlockSpec((None, m, d), lambda b: (b, 0, 0)), _const_spec((d, d)), _const_spec((d, d))],
        out_specs=[pl.BlockSpec((None, m, d), lambda b: (b, 0, 0))] * 2,
        out_shape=[jax.ShapeDtypeStruct((bsz, m, d), BF16)] * 2,
        compiler_params=_params("parallel"),
        name="mem_kv",
    )(mem, w_xk.astype(BF16), w_xv.astype(BF16))


def _in_proj_kernel(x_ref, cos_ref, sin_ref, lng_ref, lnb_ref, w_u_ref, w_cq_ref, w_ckv_ref, w_kr_ref,
                    w_g_ref, qg_ref, kvg_ref, w_q_ref, w_qs_ref, w_k_ref, w_v_ref,
                    u_ref, q_ref, k_ref, v_ref, gate_ref, *, q_scale):
    hb = _layer_norm(x_ref[...], lng_ref[...], lnb_ref[...]).astype(BF16)
    cos = cos_ref[...]
    sin = sin_ref[...]
    u_ref[...] = _dot(hb, w_u_ref[...])
    gate_ref[...] = jax.nn.sigmoid(_dot(hb, w_g_ref[...])).astype(BF16)
    kr2 = _dot(hb, w_kr_ref[...])
    k_rope = kr2[:, :LANE] * cos + kr2[:, LANE:] * sin
    cqn = _rms_norm(_dot(hb, w_cq_ref[...]), qg_ref[...]).astype(BF16)
    qa = _dot(cqn, w_q_ref[...])
    qb = _dot(cqn, w_qs_ref[...])
    ckvn = _rms_norm(_dot(hb, w_ckv_ref[...]), kvg_ref[...]).astype(BF16)
    kn = _dot(ckvn, w_k_ref[...])
    v_ref[...] = _dot(ckvn, w_v_ref[...]).astype(BF16)
    for h in range(MLA_HEADS):
        sl = slice(h * LANE, (h + 1) * LANE)
        q_ref[:, sl] = ((qa[:, sl] * cos + qb[:, sl] * sin) * q_scale).astype(BF16)
        k_ref[:, sl] = (kn[:, sl] + k_rope).astype(BF16)


def _rotate_half_cols(w):
    half = MLA_ROPE_DIM // 2
    return jnp.concatenate([-w[..., half:], w[..., :half]], axis=-1)


def _head_slab(nope, rope):
    d, h, _ = nope.shape
    pad = jnp.zeros((d, h, LANE - MLA_QK_DIM), nope.dtype)
    return jnp.concatenate([nope, rope, pad], axis=-1).reshape(d, h * LANE)


def _in_proj(x2, cos128, sin128, ln_g, ln_b, w_in, q_norm_g, w_uq, kv_norm_g, w_ukv, tm):
    n, d = x2.shape
    s5w = q_rank = kv_rank = d // 4
    o1, o2, o3 = s5w, s5w + q_rank, s5w + q_rank + kv_rank
    o4 = o3 + MLA_ROPE_DIM
    hl = MLA_HEADS * LANE
    w_u = w_in[:, :o1].astype(BF16)
    w_cq = w_in[:, o1:o2].astype(BF16)
    w_ckv = w_in[:, o2:o3].astype(BF16)
    w_kr = w_in[:, o3:o4]
    zl = jnp.zeros((d, MLA_NOPE_DIM), F32)
    zr = jnp.zeros((d, LANE - MLA_QK_DIM), F32)
    w_kr2 = jnp.concatenate([zl, w_kr, zr, zl, _rotate_half_cols(w_kr), zr], axis=-1).astype(BF16)
    w_g = w_in[:, o4:].astype(BF16)
    wq3 = w_uq.reshape(q_rank, MLA_HEADS, MLA_QK_DIM)
    wq_nope, wq_rope = wq3[..., :MLA_NOPE_DIM], wq3[..., MLA_NOPE_DIM:]
    w_q = _head_slab(wq_nope, wq_rope).astype(BF16)
    w_qs = _head_slab(jnp.zeros_like(wq_nope), _rotate_half_cols(wq_rope)).astype(BF16)
    wkv3 = w_ukv.reshape(kv_rank, MLA_HEADS, MLA_NOPE_DIM + MLA_V_DIM)
    w_k = _head_slab(wkv3[..., :MLA_NOPE_DIM],
                     jnp.zeros((kv_rank, MLA_HEADS, MLA_ROPE_DIM), F32)).astype(BF16)
    w_v = wkv3[..., MLA_NOPE_DIM:].reshape(kv_rank, MLA_HEADS * MLA_V_DIM).astype(BF16)
    row = lambda w: pl.BlockSpec((tm, w), lambda i: (i, 0))
    consts = [ln_g.reshape(1, d), ln_b.reshape(1, d), w_u, w_cq, w_ckv, w_kr2, w_g,
              q_norm_g.reshape(1, q_rank), kv_norm_g.reshape(1, kv_rank), w_q, w_qs, w_k, w_v]
    return pl.pallas_call(
        functools.partial(_in_proj_kernel, q_scale=MLA_QK_DIM ** -0.5 * math.log2(math.e)),
        grid=(n // tm,),
        in_specs=[row(d), row(LANE), row(LANE)] + [_const_spec(c.shape) for c in consts],
        out_specs=[row(s5w), row(hl), row(hl), row(MLA_HEADS * MLA_V_DIM), row(2 * d)],
        out_shape=[jax.ShapeDtypeStruct((n, s5w), F32), jax.ShapeDtypeStruct((n, hl), BF16),
                   jax.ShapeDtypeStruct((n, hl), BF16),
                   jax.ShapeDtypeStruct((n, MLA_HEADS * MLA_V_DIM), BF16),
                   jax.ShapeDtypeStruct((n, 2 * d), BF16)],
        compiler_params=_params("parallel"),
        name="in_proj",
    )(x2, cos128, sin128, *consts)


def _s5_scan_kernel(u_ref, bblk_ref, are_ref, aim_ref, cblk_ref, d_ref, wglu_ref, out_ref,
                    hbuf, st_re, st_im, *, bsz, tt, nstate):
    @pl.when(pl.program_id(0) == 0)
    def _():
        st_re[...] = jnp.zeros_like(st_re)
        st_im[...] = jnp.zeros_like(st_im)

    width = u_ref.shape[-1]
    u = jnp.swapaxes(u_ref[...], 0, 1).reshape(tt * bsz, width)
    bu = _dot(u.astype(BF16), bblk_ref[...])
    nt = nstate // LANE
    for c in range(2 * nt):
        hbuf[c] = bu[:, c * LANE:(c + 1) * LANE]
    ar = [jnp.broadcast_to(are_ref[:, c * LANE:(c + 1) * LANE], (bsz, LANE)) for c in range(nt)]
    ai = [jnp.broadcast_to(aim_ref[:, c * LANE:(c + 1) * LANE], (bsz, LANE)) for c in range(nt)]

    def step(t, carry):
        hr, hi = carry
        rows = pl.ds(pl.multiple_of(t * bsz, bsz), bsz)
        nr, ni = [], []
        for c in range(nt):
            r = ar[c] * hr[c] - ai[c] * hi[c] + hbuf[c, rows, :]
            i = ar[c] * hi[c] + ai[c] * hr[c] + hbuf[nt + c, rows, :]
            hbuf[c, rows, :] = r
            hbuf[nt + c, rows, :] = i
            nr.append(r)
            ni.append(i)
        return tuple(nr), tuple(ni)

    def steps(tb, carry):
        for k in range(SCAN_UNROLL):
            carry = step(tb * SCAN_UNROLL + k, carry)
        return carry

    init = (tuple(st_re[c] for c in range(nt)), tuple(st_im[c] for c in range(nt)))
    hr, hi = lax.fori_loop(0, tt // SCAN_UNROLL, steps, init)
    for c in range(nt):
        st_re[c] = hr[c]
        st_im[c] = hi[c]
    half = tt * bsz // 2
    ys = []
    for r in range(2):
        rows = slice(r * half, (r + 1) * half)
        hs = jnp.concatenate([hbuf[c, rows, :].astype(BF16) for c in range(2 * nt)], axis=-1)
        ys.append(_dot(hs, cblk_ref[...]))
    y = jnp.concatenate(ys, axis=0) + d_ref[...] * u
    y = jnp.swapaxes(y.reshape(tt, bsz, width), 0, 1).reshape(bsz * tt, width)
    z = _dot(_gelu_exact(y).astype(BF16), wglu_ref[...])
    dm = z.shape[-1] // 2
    s_out = z[:, :dm] * jax.nn.sigmoid(z[:, dm:])
    out_ref[...] = s_out.reshape(bsz, tt, dm).astype(BF16)


def _s5_scan(u3, ab_re, ab_im, bb_re, bb_im, c_re, c_im, d_skip, w_glu, tt):
    bsz, seq, width = u3.shape
    g, p, h = bb_re.shape
    nstate = g * p
    eye = jnp.eye(g, dtype=F32)
    blk_b = lambda bb: jnp.einsum('gph,gk->ghkp', bb, eye).reshape(g * h, nstate)
    bblk = jnp.concatenate([blk_b(bb_re), blk_b(bb_im)], axis=1).astype(BF16)
    blk_c = lambda c: jnp.einsum('ghp,gk->gpkh', c, eye).reshape(nstate, g * h)
    cblk = jnp.concatenate([blk_c(c_re), -blk_c(c_im)], axis=0).astype(BF16)
    dm2 = w_glu.shape[-1]
    consts = [bblk, ab_re.reshape(1, nstate), ab_im.reshape(1, nstate), cblk,
              d_skip.reshape(1, width), w_glu.astype(BF16)]
    return pl.pallas_call(
        functools.partial(_s5_scan_kernel, bsz=bsz, tt=tt, nstate=nstate),
        grid=(seq // tt,),
        in_specs=[pl.BlockSpec((bsz, tt, width), lambda i: (0, i, 0))] + [_const_spec(c.shape) for c in consts],
        out_specs=pl.BlockSpec((bsz, tt, dm2 // 2), lambda i: (0, i, 0)),
        out_shape=jax.ShapeDtypeStruct((bsz, seq, dm2 // 2), BF16),
        scratch_shapes=[pltpu.VMEM((2 * nstate // LANE, bsz * tt, LANE), F32),
                        pltpu.VMEM((nstate // LANE, bsz, LANE), F32),
                        pltpu.VMEM((nstate // LANE, bsz, LANE), F32)],
        compiler_params=_params("arbitrary"),
        name="s5_scan",
    )(u3, *consts)


def _attn_kernel(q_ref, k_ref, vt_ref, o_ref, *, tq, nh):
    qi = pl.program_id(2)
    heads = [slice(h * LANE, (h + 1) * LANE) for h in range(nh)]
    qs = [q_ref[:, lanes] for lanes in heads]

    def block(j, carry, masked):
        rows = pl.ds(pl.multiple_of(j * tq, tq), tq)
        sts = [_dot_nt(k_ref[rows, lanes], q) for lanes, q in zip(heads, qs)]
        if masked:
            keep = (lax.broadcasted_iota(jnp.int32, (tq, tq), 0) <= lax.broadcasted_iota(jnp.int32, (tq, tq), 1))
        out = []
        for h, ((m, l, acc), st) in enumerate(zip(carry, sts)):
            if masked:
                st = jnp.where(keep, st, NEG_INF)
            m_new = jnp.maximum(m, jnp.max(st, axis=0, keepdims=True))
            alpha = jnp.exp2(m - m_new)
            pt = jnp.exp2(st - m_new)
            l = alpha * l + jnp.sum(pt, axis=0, keepdims=True)
            vt = vt_ref[j, h * MLA_V_DIM:(h + 1) * MLA_V_DIM, :]
            acc = alpha * acc + _dot(vt, pt.astype(BF16))
            out.append((m_new, l, acc))
        return tuple(out)

    one = (jnp.full((1, tq), NEG_INF, F32), jnp.zeros((1, tq), F32), jnp.zeros((MLA_V_DIM, tq), F32))
    carry = lax.fori_loop(0, qi, lambda j, c: block(j, c, False), (one,) * nh)
    carry = block(qi, carry, True)
    ot = jnp.concatenate([acc / l for _, l, acc in carry], axis=0)
    o_ref[...] = ot.T.astype(BF16)


def _attention(q, k, v, tq, nh):
    bsz, seq, _ = q.shape
    groups = MLA_HEADS // nh
    vt = v.reshape(bsz, seq // tq, tq, groups, nh * MLA_V_DIM).transpose(0, 3, 1, 4, 2)
    return pl.pallas_call(
        functools.partial(_attn_kernel, tq=tq, nh=nh),
        grid=(bsz, groups, seq // tq),
        in_specs=[pl.BlockSpec((None, tq, nh * LANE), lambda b, h, i: (b, i, h)),
                  pl.BlockSpec((None, seq, nh * LANE), lambda b, h, i: (b, 0, h)),
                  pl.BlockSpec((None, None, seq // tq, nh * MLA_V_DIM, tq), lambda b, h, i: (b, h, 0, 0, 0))],
        out_specs=pl.BlockSpec((None, tq, nh * MLA_V_DIM), lambda b, h, i: (b, i, h)),
        out_shape=jax.ShapeDtypeStruct((bsz, seq, MLA_HEADS * MLA_V_DIM), BF16),
        compiler_params=_params("parallel", "parallel", "arbitrary"),
        name="mla_attention",
    )(q, k, vt)


def _mixer_out_kernel(x_ref, ao_ref, so_ref, gate_ref, kx_ref, vx_ref, lng_ref, lnb_ref, w_oa_ref, w_o_ref,
                      ln1g_ref, ln1b_ref, w_xq_ref, w_xo_ref, ln2g_ref, ln2b_ref, out_ref, *, x_scale):
    h0 = _layer_norm(x_ref[...], lng_ref[...], lnb_ref[...])
    d = h0.shape[-1]
    a_out = _dot(ao_ref[...], w_oa_ref[...])
    gate = gate_ref[...]
    mixed = gate[:, :d].astype(F32) * so_ref[...].astype(F32) + gate[:, d:].astype(F32) * a_out
    mix = _dot(mixed.astype(BF16), w_o_ref[...])
    h1 = _layer_norm(DN_ALPHA * h0 + mix, ln1g_ref[...], ln1b_ref[...])
    qx = (_dot(h1.astype(BF16), w_xq_ref[...]) * x_scale).astype(BF16)
    hd = d // XATTN_HEADS
    xa = jnp.zeros_like(h1)
    for hh in range(XATTN_HEADS):
        cols = slice(hh * hd, (hh + 1) * hd)
        s = _dot_nt(qx[:, cols], kx_ref[:, cols])
        p = jnp.exp(s - jnp.max(s, axis=-1, keepdims=True))
        o = _dot(p.astype(BF16), vx_ref[:, cols]) / jnp.sum(p, axis=-1, keepdims=True)
        xa = xa + _dot(o.astype(BF16), w_xo_ref[cols, :])
    out_ref[...] = _layer_norm(DN_ALPHA * h1 + xa, ln2g_ref[...], ln2b_ref[...])


def _mixer_out(x, ao, so, gate, kx, vx, ln_g, ln_b, w_oa, w_o, ln1_g, ln1_b, w_xq, w_xo, ln2_g, ln2_b, tm):
    bsz, seq, d = x.shape
    m = kx.shape[1]
    vec = lambda a: a.reshape(1, d)
    tok = lambda w: pl.BlockSpec((None, tm, w), lambda b, i: (b, i, 0))
    per_b = pl.BlockSpec((None, m, d), lambda b, i: (b, 0, 0))
    consts = [vec(ln_g), vec(ln_b), w_oa.astype(BF16), w_o.astype(BF16), vec(ln1_g), vec(ln1_b),
              w_xq.astype(BF16), w_xo.astype(BF16), vec(ln2_g), vec(ln2_b)]
    return pl.pallas_call(
        functools.partial(_mixer_out_kernel, x_scale=(d // XATTN_HEADS) ** -0.5),
        grid=(bsz, seq // tm),
        in_specs=[tok(d), tok(ao.shape[-1]), tok(d), tok(2 * d), per_b, per_b] + [_const_spec(c.shape) for c in consts],
        out_specs=tok(d),
        out_shape=jax.ShapeDtypeStruct((bsz, seq, d), F32),
        compiler_params=_params("parallel", "parallel"),
        name="mixer_out",
    )(x, ao, so, gate, kx, vx, *consts)


def _mlp_kernel(h_ref, w_up_ref, w_down_ref, g_ref, b_ref, out_ref, *, chunk):
    h = h_ref[...]
    hb = h.astype(BF16)
    hidden = w_up_ref.shape[-1]
    ff = jnp.zeros_like(h)
    for c in range(hidden // chunk):
        cols = slice(c * chunk, (c + 1) * chunk)
        a = jnp.maximum(_dot(hb, w_up_ref[:, cols]), 0.0)
        ff = ff + _dot((a * a).astype(BF16), w_down_ref[cols, :])
    out_ref[...] = _layer_norm(DN_ALPHA * h + ff, g_ref[...], b_ref[...])


def _mlp(h2, w_up, w_down, ln_g, ln_b, tm):
    n, d = h2.shape
    hidden = w_up.shape[-1]
    consts = [w_up.astype(BF16), w_down.astype(BF16), ln_g.reshape(1, d), ln_b.reshape(1, d)]
    return pl.pallas_call(
        functools.partial(_mlp_kernel, chunk=min(hidden, 1024)),
        grid=(n // tm,),
        in_specs=[pl.BlockSpec((tm, d), lambda i: (i, 0))] + [_const_spec(c.shape) for c in consts],
        out_specs=pl.BlockSpec((tm, d), lambda i: (i, 0)),
        out_shape=jax.ShapeDtypeStruct((n, d), F32),
        compiler_params=_params("parallel"),
        name="mlp",
    )(h2, *consts)


def kernel(x, mem, positions, ln_in_g, ln_in_b, w_in, s5_lam_re, s5_lam_im, s5_log_dt, s5_b_re, s5_b_im,
           s5_c_re, s5_c_im, s5_d, w_glu, q_norm_g, w_uq, kv_norm_g, w_ukv, w_oa, w_o, ln1_g, ln1_b,
           w_xq, w_xk, w_xv, w_xo, ln2_g, ln2_b, w_up, w_down, ln3_g, ln3_b):
    bsz, seq, d = x.shape
    n = bsz * seq
    assert w_in.shape[0] == DEPTH == 1
    tm = min(seq, 512)
    tq = min(seq, 512)
    nh = 4
    tt = min(seq, 128)

    cos, sin = _rope_tables(positions)
    ones = jnp.ones((bsz, seq, MLA_NOPE_DIM), F32)
    zeros = jnp.zeros((bsz, seq, LANE - MLA_QK_DIM), F32)
    cos128 = jnp.concatenate([ones, cos, cos, zeros], axis=-1).reshape(n, LANE)
    sin128 = jnp.concatenate([0.0 * ones, sin, sin, zeros], axis=-1).reshape(n, LANE)

    u, q, k, v, gate = _in_proj(x.reshape(n, d), cos128, sin128, ln_in_g, ln_in_b, w_in[0],
                                q_norm_g[0], w_uq[0], kv_norm_g[0], w_ukv[0], tm)

    ab_re, ab_im, bb_re, bb_im = _s5_discretize(s5_lam_re[0], s5_lam_im[0], s5_log_dt[0], s5_b_re[0], s5_b_im[0])
    s_out = _s5_scan(u.reshape(bsz, seq, -1), ab_re, ab_im, bb_re, bb_im, s5_c_re[0], s5_c_im[0],
                     s5_d[0], w_glu[0], tt)

    hl = MLA_HEADS * LANE
    a_o = _attention(q.reshape(bsz, seq, hl), k.reshape(bsz, seq, hl), v.reshape(bsz, seq, -1), tq, nh)

    kx, vx = _mem_kv(mem, w_xk[0], w_xv[0])
    h2 = _mixer_out(x, a_o, s_out, gate.reshape(bsz, seq, 2 * d), kx, vx, ln_in_g, ln_in_b, w_oa[0], w_o[0],
                    ln1_g[0], ln1_b[0], w_xq[0], w_xo[0], ln2_g[0], ln2_b[0], tm)
    out = _mlp(h2.reshape(n, d), w_up[0], w_down[0], ln3_g[0], ln3_b[0], tm)
    return out.reshape(bsz, seq, d)
```

```python
import functools
import math

import jax
import jax.numpy as jnp
from jax import lax
from jax.experimental import pallas as pl
from jax.experimental.pallas import tpu as pltpu

F32 = jnp.float32
BF16 = jnp.bfloat16

S5_GROUP_CH = 16
S5_STATE = 64
S5_MAX_RE = -1e-4
MLA_HEADS = 8
MLA_NOPE_DIM = 64
MLA_ROPE_DIM = 32
MLA_QK_DIM = MLA_NOPE_DIM + MLA_ROPE_DIM
MLA_V_DIM = 64
ROPE_THETA = 10000.0
XATTN_HEADS = 4
LN_EPS = 1e-5
RMS_EPS = 1e-6
NEG_INF = -1e30
DEPTH = 1
DN_ALPHA = (2.0 * DEPTH) ** 0.25

SCAN_UNROLL = 4
MAX_CHAINS = 4
QK_AHEAD = 2
F32_SUBLANES = 8
BF16_SUBLANES = 16
VT_ROWS = MLA_V_DIM + BF16_SUBLANES
LANE = 128
VMEM_LIMIT = 56 * 1024 * 1024


def _const_spec(shape):
    nd = len(shape)
    return pl.BlockSpec(shape, lambda *_: (0,) * nd, pipeline_mode=pl.Buffered(1))


def _params(*sem, flags=None):
    return pltpu.CompilerParams(dimension_semantics=sem, vmem_limit_bytes=VMEM_LIMIT, flags=flags)


def _layer_norm(x, g, b):
    mu = jnp.mean(x, axis=-1, keepdims=True)
    xc = x - mu
    var = jnp.mean(xc * xc, axis=-1, keepdims=True)
    return xc * lax.rsqrt(var + LN_EPS) * g + b


def _rms_norm(x, g):
    return x * lax.rsqrt(jnp.mean(x * x, axis=-1, keepdims=True) + RMS_EPS) * g


def _gelu_exact(x):
    return 0.5 * x * (1.0 + lax.erf(x * (0.5 ** 0.5)))


def _dot(a, b):
    return jnp.dot(a, b, preferred_element_type=F32)


def _dot_nt(a, b):
    return lax.dot_general(a, b, (((1,), (1,)), ((), ())), preferred_element_type=F32)


def _s5_disc_kernel(lr_ref, li_ref, ldt_ref, br_ref, bi_ref, are_ref, aim_ref, bbr_ref, bbi_ref):
    lr = jnp.minimum(lr_ref[...], S5_MAX_RE)
    li = li_ref[...]
    dt = jnp.exp(ldt_ref[...])
    mag = jnp.exp(lr * dt)
    ang = li * dt
    ab_re = mag * jnp.cos(ang)
    ab_im = mag * jnp.sin(ang)
    den = lr * lr + li * li
    nr = ab_re - 1.0
    f_re = (nr * lr + ab_im * li) / den
    f_im = (ab_im * lr - nr * li) / den
    br = br_ref[...]
    bi = bi_ref[...]
    are_ref[...] = ab_re
    aim_ref[...] = ab_im
    bbr_ref[...] = f_re * br - f_im * bi
    bbi_ref[...] = f_re * bi + f_im * br


def _s5_discretize(lam_re, lam_im, log_dt, b_re, b_im):
    g, p = lam_re.shape
    h = b_re.shape[-1]
    n = g * p
    col = lambda a: a.reshape(n, 1)
    ldt = jnp.broadcast_to(log_dt[:, None], (g, p))
    outs = pl.pallas_call(
        _s5_disc_kernel,
        out_shape=[jax.ShapeDtypeStruct((n, 1), F32), jax.ShapeDtypeStruct((n, 1), F32),
                   jax.ShapeDtypeStruct((n, h), F32), jax.ShapeDtypeStruct((n, h), F32)],
        name="s5_discretize",
    )(col(lam_re), col(lam_im), col(ldt), b_re.reshape(n, h), b_im.reshape(n, h))
    ab_re, ab_im, bb_re, bb_im = outs
    return ab_re.reshape(g, p), ab_im.reshape(g, p), bb_re.reshape(g, p, h), bb_im.reshape(g, p, h)


def _rope_kernel(pos_ref, inv_ref, cos_ref, sin_ref):
    ang = pos_ref[...].astype(F32) * inv_ref[...]
    cos_ref[...] = jnp.cos(ang)
    sin_ref[...] = jnp.sin(ang)


def _rope_tables(positions):
    bsz, seq = positions.shape
    half = MLA_ROPE_DIM // 2
    per_row = LANE // half
    rows = bsz * seq // per_row
    inv = ROPE_THETA ** (-jnp.arange(0, MLA_ROPE_DIM, 2, dtype=F32) / MLA_ROPE_DIM)
    inv_row = jnp.tile(inv, per_row).reshape(1, LANE)
    pos_rep = jnp.repeat(positions.reshape(-1), half).reshape(rows, LANE)
    tr = min(rows, 1024)
    cos, sin = pl.pallas_call(
        _rope_kernel,
        grid=(rows // tr,),
        in_specs=[pl.BlockSpec((tr, LANE), lambda i: (i, 0)), _const_spec((1, LANE))],
        out_specs=[pl.BlockSpec((tr, LANE), lambda i: (i, 0))] * 2,
        out_shape=[jax.ShapeDtypeStruct((rows, LANE), F32)] * 2,
        compiler_params=_params("parallel"),
        name="rope_table",
    )(pos_rep, inv_row)
    return cos.reshape(bsz, seq, half), sin.reshape(bsz, seq, half)


def _mem_kv_kernel(mem_ref, wk_ref, wv_ref, k_ref, v_ref):
    m = mem_ref[...].astype(BF16)
    k_ref[...] = _dot(m, wk_ref[...]).astype(BF16)
    v_ref[...] = _dot(m, wv_ref[...]).astype(BF16)


def _mem_kv(mem, w_xk, w_xv):
    bsz, m, d = mem.shape
    return pl.pallas_call(
        _mem_kv_kernel,
        grid=(bsz,),
        in_specs=[pl.BlockSpec((None, m, d), lambda b: (b, 0, 0)), _const_spec((d, d)), _const_spec((d, d))],
        out_specs=[pl.BlockSpec((None, m, d), lambda b: (b, 0, 0))] * 2,
        out_shape=[jax.ShapeDtypeStruct((bsz, m, d), BF16)] * 2,
        compiler_params=_params("parallel"),
        name="mem_kv",
    )(mem, w_xk.astype(BF16), w_xv.astype(BF16))


def _in_proj_kernel(x_ref, cos_ref, sin_ref, lng_ref, lnb_ref, w_u_ref, w_cq_ref, w_ckv_ref, w_kr_ref,
                    w_g_ref, qg_ref, kvg_ref, w_q_ref, w_qs_ref, w_k_ref, w_vt_ref,
                    u_ref, q_ref, k_ref, vt_ref, gate_ref, *, q_scale, nh):
    hb = _layer_norm(x_ref[...], lng_ref[...], lnb_ref[...]).astype(BF16)
    cos = cos_ref[...]
    sin = sin_ref[...]
    u_ref[...] = _dot(hb, w_u_ref[...])
    gate_ref[...] = jax.nn.sigmoid(_dot(hb, w_g_ref[...])).astype(BF16)
    kr2 = _dot(hb, w_kr_ref[...])
    k_rope = kr2[:, :LANE] * cos + kr2[:, LANE:] * sin
    cqn = _rms_norm(_dot(hb, w_cq_ref[...]), qg_ref[...]).astype(BF16)
    qa = _dot(cqn, w_q_ref[...])
    qb = _dot(cqn, w_qs_ref[...])
    ckvn = _rms_norm(_dot(hb, w_ckv_ref[...]), kvg_ref[...]).astype(BF16)
    kn = _dot(ckvn, w_k_ref[...])
    vt = _dot_nt(w_vt_ref[...], ckvn)
    ones = jnp.ones((VT_ROWS - MLA_V_DIM, vt.shape[-1]), BF16)
    for h in range(MLA_HEADS):
        g, r0 = h // nh, (h % nh) * VT_ROWS
        vt_ref[g, r0:r0 + MLA_V_DIM, :] = vt[h * MLA_V_DIM:(h + 1) * MLA_V_DIM, :].astype(BF16)
        vt_ref[g, r0 + MLA_V_DIM:r0 + VT_ROWS, :] = ones
    for h in range(MLA_HEADS):
        sl = slice(h * LANE, (h + 1) * LANE)
        q_ref[:, sl] = ((qa[:, sl] * cos + qb[:, sl] * sin) * q_scale).astype(BF16)
        k_ref[:, sl] = (kn[:, sl] + k_rope).astype(BF16)


def _rotate_half_cols(w):
    half = MLA_ROPE_DIM // 2
    return jnp.concatenate([-w[..., half:], w[..., :half]], axis=-1)


def _head_slab(nope, rope):
    d, h, _ = nope.shape
    pad = jnp.zeros((d, h, LANE - MLA_QK_DIM), nope.dtype)
    return jnp.concatenate([nope, rope, pad], axis=-1).reshape(d, h * LANE)


def _in_proj(x, cos128, sin128, ln_g, ln_b, w_in, q_norm_g, w_uq, kv_norm_g, w_ukv, tm, nh):
    bsz, seq, d = x.shape
    groups = MLA_HEADS // nh
    s5w = q_rank = kv_rank = d // 4
    o1, o2, o3 = s5w, s5w + q_rank, s5w + q_rank + kv_rank
    o4 = o3 + MLA_ROPE_DIM
    hl = MLA_HEADS * LANE
    w_u = w_in[:, :o1].astype(BF16)
    w_cq = w_in[:, o1:o2].astype(BF16)
    w_ckv = w_in[:, o2:o3].astype(BF16)
    w_kr = w_in[:, o3:o4]
    zl = jnp.zeros((d, MLA_NOPE_DIM), F32)
    zr = jnp.zeros((d, LANE - MLA_QK_DIM), F32)
    w_kr2 = jnp.concatenate([zl, w_kr, zr, zl, _rotate_half_cols(w_kr), zr], axis=-1).astype(BF16)
    w_g = w_in[:, o4:].astype(BF16)
    wq3 = w_uq.reshape(q_rank, MLA_HEADS, MLA_QK_DIM)
    wq_nope, wq_rope = wq3[..., :MLA_NOPE_DIM], wq3[..., MLA_NOPE_DIM:]
    w_q = _head_slab(wq_nope, wq_rope).astype(BF16)
    w_qs = _head_slab(jnp.zeros_like(wq_nope), _rotate_half_cols(wq_rope)).astype(BF16)
    wkv3 = w_ukv.reshape(kv_rank, MLA_HEADS, MLA_NOPE_DIM + MLA_V_DIM)
    w_k = _head_slab(wkv3[..., :MLA_NOPE_DIM],
                     jnp.zeros((kv_rank, MLA_HEADS, MLA_ROPE_DIM), F32)).astype(BF16)
    w_vt = wkv3[..., MLA_NOPE_DIM:].reshape(kv_rank, MLA_HEADS * MLA_V_DIM).T.astype(BF16)
    row = lambda w: pl.BlockSpec((None, tm, w), lambda b, i: (b, i, 0))
    tok = lambda w, dt: jax.ShapeDtypeStruct((bsz, seq, w), dt)
    consts = [ln_g.reshape(1, d), ln_b.reshape(1, d), w_u, w_cq, w_ckv, w_kr2, w_g,
              q_norm_g.reshape(1, q_rank), kv_norm_g.reshape(1, kv_rank), w_q, w_qs, w_k, w_vt]
    return pl.pallas_call(
        functools.partial(_in_proj_kernel, q_scale=MLA_QK_DIM ** -0.5 * math.log2(math.e), nh=nh),
        grid=(bsz, seq // tm),
        in_specs=[row(d), row(LANE), row(LANE)] + [_const_spec(c.shape) for c in consts],
        out_specs=[row(s5w), row(hl), row(hl),
                   pl.BlockSpec((None, None, groups, nh * VT_ROWS, tm), lambda b, i: (b, i, 0, 0, 0)), row(2 * d)],
        out_shape=[tok(s5w, F32), tok(hl, BF16), tok(hl, BF16),
                   jax.ShapeDtypeStruct((bsz, seq // tm, groups, nh * VT_ROWS, tm), BF16), tok(2 * d, BF16)],
        compiler_params=_params("parallel", "parallel"),
        name="in_proj",
    )(x, cos128, sin128, *consts)


def _s5_scan_kernel(u_ref, bblk_ref, are_ref, aim_ref, cblk_ref, d_ref, wglu_ref, out_ref,
                    hbuf, st_re, st_im, *, bsz, tt, nstate):
    @pl.when(pl.program_id(0) == 0)
    def _():
        st_re[...] = jnp.zeros_like(st_re)
        st_im[...] = jnp.zeros_like(st_im)

    width = u_ref.shape[-1]
    u = jnp.swapaxes(u_ref[...], 0, 1).reshape(tt * bsz, width)
    bu = _dot(u.astype(BF16), bblk_ref[...])
    nt = nstate // LANE
    for c in range(2 * nt):
        hbuf[c] = bu[:, c * LANE:(c + 1) * LANE]
    ar = [jnp.broadcast_to(are_ref[:, c * LANE:(c + 1) * LANE], (bsz, LANE)) for c in range(nt)]
    ai = [jnp.broadcast_to(aim_ref[:, c * LANE:(c + 1) * LANE], (bsz, LANE)) for c in range(nt)]

    def step(t, carry):
        hr, hi = carry
        rows = pl.ds(pl.multiple_of(t * bsz, bsz), bsz)
        nr, ni = [], []
        for c in range(nt):
            r = ar[c] * hr[c] - ai[c] * hi[c] + hbuf[c, rows, :]
            i = ar[c] * hi[c] + ai[c] * hr[c] + hbuf[nt + c, rows, :]
            hbuf[c, rows, :] = r
            hbuf[nt + c, rows, :] = i
            nr.append(r)
            ni.append(i)
        return tuple(nr), tuple(ni)

    def steps(tb, carry):
        for k in range(SCAN_UNROLL):
            carry = step(tb * SCAN_UNROLL + k, carry)
        return carry

    init = (tuple(st_re[c] for c in range(nt)), tuple(st_im[c] for c in range(nt)))
    hr, hi = lax.fori_loop(0, tt // SCAN_UNROLL, steps, init)
    for c in range(nt):
        st_re[c] = hr[c]
        st_im[c] = hi[c]
    half = tt * bsz // 2
    ys = []
    for r in range(2):
        rows = slice(r * half, (r + 1) * half)
        hs = jnp.concatenate([hbuf[c, rows, :].astype(BF16) for c in range(2 * nt)], axis=-1)
        ys.append(_dot(hs, cblk_ref[...]))
    y = jnp.concatenate(ys, axis=0) + d_ref[...] * u
    y = jnp.swapaxes(y.reshape(tt, bsz, width), 0, 1).reshape(bsz * tt, width)
    z = _dot(_gelu_exact(y).astype(BF16), wglu_ref[...])
    dm = z.shape[-1] // 2
    s_out = z[:, :dm] * jax.nn.sigmoid(z[:, dm:])
    out_ref[...] = s_out.reshape(bsz, tt, dm).astype(BF16)


def _s5_scan(u3, ab_re, ab_im, bb_re, bb_im, c_re, c_im, d_skip, w_glu, tt):
    bsz, seq, width = u3.shape
    g, p, h = bb_re.shape
    nstate = g * p
    eye = jnp.eye(g, dtype=F32)
    blk_b = lambda bb: jnp.einsum('gph,gk->ghkp', bb, eye).reshape(g * h, nstate)
    bblk = jnp.concatenate([blk_b(bb_re), blk_b(bb_im)], axis=1).astype(BF16)
    blk_c = lambda c: jnp.einsum('ghp,gk->gpkh', c, eye).reshape(nstate, g * h)
    cblk = jnp.concatenate([blk_c(c_re), -blk_c(c_im)], axis=0).astype(BF16)
    dm2 = w_glu.shape[-1]
    consts = [bblk, ab_re.reshape(1, nstate), ab_im.reshape(1, nstate), cblk,
              d_skip.reshape(1, width), w_glu.astype(BF16)]
    return pl.pallas_call(
        functools.partial(_s5_scan_kernel, bsz=bsz, tt=tt, nstate=nstate),
        grid=(seq // tt,),
        in_specs=[pl.BlockSpec((bsz, tt, width), lambda i: (0, i, 0))] + [_const_spec(c.shape) for c in consts],
        out_specs=pl.BlockSpec((bsz, tt, dm2 // 2), lambda i: (0, i, 0)),
        out_shape=jax.ShapeDtypeStruct((bsz, seq, dm2 // 2), BF16),
        scratch_shapes=[pltpu.VMEM((2 * nstate // LANE, bsz * tt, LANE), F32),
                        pltpu.VMEM((nstate // LANE, bsz, LANE), F32),
                        pltpu.VMEM((nstate // LANE, bsz, LANE), F32)],
        compiler_params=_params("arbitrary"),
        name="s5_scan",
    )(u3, *consts)


def _attn_kernel(q_ref, k_ref, vt_ref, o_ref, s_scr, p_scr, acc_scr, *, tq, nh, kb):
    qi = pl.program_id(2)
    heads = [slice(h * LANE, (h + 1) * LANE) for h in range(nh)]
    acc_scr[...] = jnp.zeros_like(acc_scr)

    def block(j, nblk, ms, masked):
        nkeys = nblk * tq
        rows = pl.ds(pl.multiple_of(j * tq, tq), nkeys)

        def scores(h):
            s = _dot_nt(k_ref[rows, heads[h]], q_ref[:, heads[h]])
            if masked:
                keep = (lax.broadcasted_iota(jnp.int32, (tq, tq), 0) <= lax.broadcasted_iota(jnp.int32, (tq, tq), 1))
                s = jnp.where(keep, s, NEG_INF)
            s_scr[h, :nkeys, :] = s

        for h in range(min(QK_AHEAD, nh)):
            scores(h)
        new_ms = []
        for h in range(nh):
            if h + QK_AHEAD < nh:
                scores(h + QK_AHEAD)
            tiles = [s_scr[h, r * F32_SUBLANES:(r + 1) * F32_SUBLANES, :] for r in range(nkeys // F32_SUBLANES)]
            part = tiles[:MAX_CHAINS]
            for r, t in enumerate(tiles[MAX_CHAINS:]):
                part[r % MAX_CHAINS] = jnp.maximum(part[r % MAX_CHAINS], t)
            m8 = functools.reduce(jnp.maximum, part)
            m_new = jnp.maximum(ms[h], jnp.max(m8, axis=0, keepdims=True))
            for r in range(nkeys // BF16_SUBLANES):
                sl = slice(r * BF16_SUBLANES, (r + 1) * BF16_SUBLANES)
                p_scr[h, sl, :] = jnp.exp2((s_scr[h, sl, :] - m_new).astype(BF16))
            acc = jnp.exp2(ms[h] - m_new) * acc_scr[h]
            for i in range(nblk):
                vt = vt_ref[j + i, h * VT_ROWS:(h + 1) * VT_ROWS, :]
                acc = acc + _dot(vt, p_scr[h, i * tq:(i + 1) * tq, :])
            acc_scr[h] = acc
            new_ms.append(m_new)
        return tuple(new_ms)

    ms = (jnp.full((1, tq), NEG_INF, F32),) * nh
    nmain = qi // kb
    ms = lax.fori_loop(0, nmain, lambda t, c: block(t * kb, kb, c, False), ms)
    ms = lax.fori_loop(nmain * kb, qi, lambda j, c: block(j, 1, c, False), ms)
    block(qi, 1, ms, True)
    ot = jnp.concatenate([acc_scr[h, :MLA_V_DIM, :] / acc_scr[h, MLA_V_DIM:MLA_V_DIM + 1, :] for h in range(nh)],
                         axis=0)
    o_ref[...] = ot.T.astype(BF16)


def _attention(q, k, vt, tq, nh, kb):
    bsz, seq, _ = q.shape
    groups = MLA_HEADS // nh
    return pl.pallas_call(
        functools.partial(_attn_kernel, tq=tq, nh=nh, kb=kb),
        grid=(bsz, groups, seq // tq),
        in_specs=[pl.BlockSpec((None, tq, nh * LANE), lambda b, h, i: (b, i, h)),
                  pl.BlockSpec((None, seq, nh * LANE), lambda b, h, i: (b, 0, h)),
                  pl.BlockSpec((None, seq // tq, None, nh * VT_ROWS, tq), lambda b, h, i: (b, 0, h, 0, 0))],
        out_specs=pl.BlockSpec((None, tq, nh * MLA_V_DIM), lambda b, h, i: (b, i, h)),
        out_shape=jax.ShapeDtypeStruct((bsz, seq, MLA_HEADS * MLA_V_DIM), BF16),
        scratch_shapes=[pltpu.VMEM((nh, kb * tq, tq), F32), pltpu.VMEM((nh, kb * tq, tq), BF16),
                        pltpu.VMEM((nh, VT_ROWS, tq), F32)],
        compiler_params=_params("parallel", "parallel", "arbitrary"),
        name="mla_attention",
    )(q, k, vt)


def _layer_tail_kernel(x_ref, ao_ref, so_ref, gate_ref, kx_ref, vx_ref, lng_ref, lnb_ref, w_oa_ref, w_o_ref,
                       ln1g_ref, ln1b_ref, w_xq_ref, w_xo_ref, ln2g_ref, ln2b_ref, w_up_ref, w_down_ref,
                       ln3g_ref, ln3b_ref, out_ref, *, x_scale, chunk):
    h0 = _layer_norm(x_ref[...], lng_ref[...], lnb_ref[...])
    d = h0.shape[-1]
    a_out = _dot(ao_ref[...], w_oa_ref[...])
    gate = gate_ref[...]
    mixed = gate[:, :d].astype(F32) * so_ref[...].astype(F32) + gate[:, d:].astype(F32) * a_out
    mix = _dot(mixed.astype(BF16), w_o_ref[...])
    h1 = _layer_norm(DN_ALPHA * h0 + mix, ln1g_ref[...], ln1b_ref[...])
    qx = (_dot(h1.astype(BF16), w_xq_ref[...]) * x_scale).astype(BF16)
    hd = d // XATTN_HEADS
    xa = jnp.zeros_like(h1)
    for hh in range(XATTN_HEADS):
        cols = slice(hh * hd, (hh + 1) * hd)
        s = _dot_nt(qx[:, cols], kx_ref[:, cols])
        p = jnp.exp(s - jnp.max(s, axis=-1, keepdims=True))
        o = _dot(p.astype(BF16), vx_ref[:, cols]) / jnp.sum(p, axis=-1, keepdims=True)
        xa = xa + _dot(o.astype(BF16), w_xo_ref[cols, :])
    h2 = _layer_norm(DN_ALPHA * h1 + xa, ln2g_ref[...], ln2b_ref[...])
    hb = h2.astype(BF16)
    ff = jnp.zeros_like(h2)
    for c in range(w_up_ref.shape[-1] // chunk):
        cols = slice(c * chunk, (c + 1) * chunk)
        a = jnp.maximum(_dot(hb, w_up_ref[:, cols]), 0.0)
        ff = ff + _dot((a * a).astype(BF16), w_down_ref[cols, :])
    out_ref[...] = _layer_norm(DN_ALPHA * h2 + ff, ln3g_ref[...], ln3b_ref[...])


def _layer_tail(x, ao, so, gate, kx, vx, ln_g, ln_b, w_oa, w_o, ln1_g, ln1_b, w_xq, w_xo, ln2_g, ln2_b,
                w_up, w_down, ln3_g, ln3_b, tm):
    bsz, seq, d = x.shape
    m = kx.shape[1]
    hidden = w_up.shape[-1]
    vec = lambda a: a.reshape(1, d)
    tok = lambda w: pl.BlockSpec((None, tm, w), lambda b, i: (b, i, 0))
    per_b = pl.BlockSpec((None, m, d), lambda b, i: (b, 0, 0))
    consts = [vec(ln_g), vec(ln_b), w_oa.astype(BF16), w_o.astype(BF16), vec(ln1_g), vec(ln1_b),
              w_xq.astype(BF16), w_xo.astype(BF16), vec(ln2_g), vec(ln2_b),
              w_up.astype(BF16), w_down.astype(BF16), vec(ln3_g), vec(ln3_b)]
    return pl.pallas_call(
        functools.partial(_layer_tail_kernel, x_scale=(d // XATTN_HEADS) ** -0.5, chunk=min(hidden, 1024)),
        grid=(bsz, seq // tm),
        in_specs=[tok(d), tok(ao.shape[-1]), tok(d), tok(2 * d), per_b, per_b] + [_const_spec(c.shape) for c in consts],
        out_specs=tok(d),
        out_shape=jax.ShapeDtypeStruct((bsz, seq, d), F32),
        compiler_params=_params("parallel", "parallel"),
        name="layer_tail",
    )(x, ao, so, gate, kx, vx, *consts)


def kernel(x, mem, positions, ln_in_g, ln_in_b, w_in, s5_lam_re, s5_lam_im, s5_log_dt, s5_b_re, s5_b_im,
           s5_c_re, s5_c_im, s5_d, w_glu, q_norm_g, w_uq, kv_norm_g, w_ukv, w_oa, w_o, ln1_g, ln1_b,
           w_xq, w_xk, w_xv, w_xo, ln2_g, ln2_b, w_up, w_down, ln3_g, ln3_b):
    bsz, seq, d = x.shape
    n = bsz * seq
    assert w_in.shape[0] == DEPTH == 1
    tm = min(seq, 512)
    tq = min(seq, 512)
    nh = 4
    kb = 2
    tt = min(seq, 128)

    cos, sin = _rope_tables(positions)
    ones = jnp.ones((bsz, seq, MLA_NOPE_DIM), F32)
    zeros = jnp.zeros((bsz, seq, LANE - MLA_QK_DIM), F32)
    cos128 = jnp.concatenate([ones, cos, cos, zeros], axis=-1)
    sin128 = jnp.concatenate([0.0 * ones, sin, sin, zeros], axis=-1)

    assert tm == tq
    u, q, k, vt, gate = _in_proj(x, cos128, sin128, ln_in_g, ln_in_b, w_in[0],
                                 q_norm_g[0], w_uq[0], kv_norm_g[0], w_ukv[0], tm, nh)

    ab_re, ab_im, bb_re, bb_im = _s5_discretize(s5_lam_re[0], s5_lam_im[0], s5_log_dt[0], s5_b_re[0], s5_b_im[0])
    s_out = _s5_scan(u, ab_re, ab_im, bb_re, bb_im, s5_c_re[0], s5_c_im[0],
                     s5_d[0], w_glu[0], tt)

    a_o = _attention(q, k, vt, tq, nh, kb)

    kx, vx = _mem_kv(mem, w_xk[0], w_xv[0])
    return _layer_tail(x, a_o, s_out, gate, kx, vx, ln_in_g, ln_in_b, w_oa[0], w_o[0], ln1_g[0], ln1_b[0],
                       w_xq[0], w_xo[0], ln2_g[0], ln2_b[0], w_up[0], w_down[0], ln3_g[0], ln3_b[0], tm)
```

```python
import functools
import math

import jax
import jax.numpy as jnp
from jax import lax
from jax.experimental import pallas as pl
from jax.experimental.pallas import tpu as pltpu

F32 = jnp.float32
BF16 = jnp.bfloat16

S5_GROUP_CH = 16
S5_STATE = 64
S5_MAX_RE = -1e-4
MLA_HEADS = 8
MLA_NOPE_DIM = 64
MLA_ROPE_DIM = 32
MLA_QK_DIM = MLA_NOPE_DIM + MLA_ROPE_DIM
MLA_V_DIM = 64
ROPE_THETA = 10000.0
XATTN_HEADS = 4
LN_EPS = 1e-5
RMS_EPS = 1e-6
NEG_INF = -1e30
DEPTH = 1
DN_ALPHA = (2.0 * DEPTH) ** 0.25

SCAN_UNROLL = 4
MAX_CHAINS = 4
QK_AHEAD = 2
F32_SUBLANES = 8
BF16_SUBLANES = 16
VT_ROWS = MLA_V_DIM + BF16_SUBLANES
LANE = 128
VMEM_LIMIT = 56 * 1024 * 1024


def _const_spec(shape):
    nd = len(shape)
    return pl.BlockSpec(shape, lambda *_: (0,) * nd, pipeline_mode=pl.Buffered(1))


def _params(*sem, flags=None):
    return pltpu.CompilerParams(dimension_semantics=sem, vmem_limit_bytes=VMEM_LIMIT, flags=flags)


def _layer_norm(x, g, b):
    mu = jnp.mean(x, axis=-1, keepdims=True)
    xc = x - mu
    var = jnp.mean(xc * xc, axis=-1, keepdims=True)
    return xc * lax.rsqrt(var + LN_EPS) * g + b


def _rms_norm(x, g):
    return x * lax.rsqrt(jnp.mean(x * x, axis=-1, keepdims=True) + RMS_EPS) * g


def _gelu_exact(x):
    return 0.5 * x * (1.0 + lax.erf(x * (0.5 ** 0.5)))


def _dot(a, b):
    return jnp.dot(a, b, preferred_element_type=F32)


def _dot_nt(a, b):
    return lax.dot_general(a, b, (((1,), (1,)), ((), ())), preferred_element_type=F32)


def _s5_disc_kernel(lr_ref, li_ref, ldt_ref, br_ref, bi_ref, are_ref, aim_ref, bbr_ref, bbi_ref):
    lr = jnp.minimum(lr_ref[...], S5_MAX_RE)
    li = li_ref[...]
    dt = jnp.exp(ldt_ref[...])
    mag = jnp.exp(lr * dt)
    ang = li * dt
    ab_re = mag * jnp.cos(ang)
    ab_im = mag * jnp.sin(ang)
    den = lr * lr + li * li
    nr = ab_re - 1.0
    f_re = (nr * lr + ab_im * li) / den
    f_im = (ab_im * lr - nr * li) / den
    br = br_ref[...]
    bi = bi_ref[...]
    are_ref[...] = ab_re
    aim_ref[...] = ab_im
    bbr_ref[...] = f_re * br - f_im * bi
    bbi_ref[...] = f_re * bi + f_im * br


def _s5_discretize(lam_re, lam_im, log_dt, b_re, b_im):
    g, p = lam_re.shape
    h = b_re.shape[-1]
    n = g * p
    col = lambda a: a.reshape(n, 1)
    ldt = jnp.broadcast_to(log_dt[:, None], (g, p))
    outs = pl.pallas_call(
        _s5_disc_kernel,
        out_shape=[jax.ShapeDtypeStruct((n, 1), F32), jax.ShapeDtypeStruct((n, 1), F32),
                   jax.ShapeDtypeStruct((n, h), F32), jax.ShapeDtypeStruct((n, h), F32)],
        name="s5_discretize",
    )(col(lam_re), col(lam_im), col(ldt), b_re.reshape(n, h), b_im.reshape(n, h))
    ab_re, ab_im, bb_re, bb_im = outs
    return ab_re.reshape(g, p), ab_im.reshape(g, p), bb_re.reshape(g, p, h), bb_im.reshape(g, p, h)


def _rope_kernel(pos_ref, inv_ref, cos_ref, sin_ref):
    ang = pos_ref[...].astype(F32) * inv_ref[...]
    cos_ref[...] = jnp.cos(ang)
    sin_ref[...] = jnp.sin(ang)


def _rope_tables(positions):
    bsz, seq = positions.shape
    half = MLA_ROPE_DIM // 2
    per_row = LANE // half
    rows = bsz * seq // per_row
    inv = ROPE_THETA ** (-jnp.arange(0, MLA_ROPE_DIM, 2, dtype=F32) / MLA_ROPE_DIM)
    inv_row = jnp.tile(inv, per_row).reshape(1, LANE)
    pos_rep = jnp.repeat(positions.reshape(-1), half).reshape(rows, LANE)
    tr = min(rows, 1024)
    cos, sin = pl.pallas_call(
        _rope_kernel,
        grid=(rows // tr,),
        in_specs=[pl.BlockSpec((tr, LANE), lambda i: (i, 0)), _const_spec((1, LANE))],
        out_specs=[pl.BlockSpec((tr, LANE), lambda i: (i, 0))] * 2,
        out_shape=[jax.ShapeDtypeStruct((rows, LANE), F32)] * 2,
        compiler_params=_params("parallel"),
        name="rope_table",
    )(pos_rep, inv_row)
    return cos.reshape(bsz, seq, half), sin.reshape(bsz, seq, half)


def _mem_kv_kernel(mem_ref, wk_ref, wv_ref, k_ref, v_ref):
    m = mem_ref[...].astype(BF16)
    k_ref[...] = _dot(m, wk_ref[...]).astype(BF16)
    v_ref[...] = _dot(m, wv_ref[...]).astype(BF16)


def _mem_kv(mem, w_xk, w_xv):
    bsz, m, d = mem.shape
    return pl.pallas_call(
        _mem_kv_kernel,
        grid=(bsz,),
        in_specs=[pl.BlockSpec((None, m, d), lambda b: (b, 0, 0)), _const_spec((d, d)), _const_spec((d, d))],
        out_specs=[pl.BlockSpec((None, m, d), lambda b: (b, 0, 0))] * 2,
        out_shape=[jax.ShapeDtypeStruct((bsz, m, d), BF16)] * 2,
        compiler_params=_params("parallel"),
        name="mem_kv",
    )(mem, w_xk.astype(BF16), w_xv.astype(BF16))


def _in_proj_kernel(x_ref, cos_ref, sin_ref, lng_ref, lnb_ref, w_u_ref, w_cq_ref, w_ckv_ref, w_kr_ref,
                    w_g_ref, qg_ref, kvg_ref, w_q_ref, w_qs_ref, w_k_ref, w_vt_ref,
                    u_ref, q_ref, k_ref, vt_ref, gate_ref, *, q_scale, nh):
    hb = _layer_norm(x_ref[...], lng_ref[...], lnb_ref[...]).astype(BF16)
    cos = cos_ref[...]
    sin = sin_ref[...]
    u_ref[...] = _dot(hb, w_u_ref[...])
    gate_ref[...] = jax.nn.sigmoid(_dot(hb, w_g_ref[...])).astype(BF16)
    kr2 = _dot(hb, w_kr_ref[...])
    k_rope = kr2[:, :LANE] * cos + kr2[:, LANE:] * sin
    cqn = _rms_norm(_dot(hb, w_cq_ref[...]), qg_ref[...]).astype(BF16)
    qa = _dot(cqn, w_q_ref[...])
    qb = _dot(cqn, w_qs_ref[...])
    ckvn = _rms_norm(_dot(hb, w_ckv_ref[...]), kvg_ref[...]).astype(BF16)
    kn = _dot(ckvn, w_k_ref[...])
    vt = _dot_nt(w_vt_ref[...], ckvn)
    ones = jnp.ones((VT_ROWS - MLA_V_DIM, vt.shape[-1]), BF16)
    for h in range(MLA_HEADS):
        g, r0 = h // nh, (h % nh) * VT_ROWS
        vt_ref[g, r0:r0 + MLA_V_DIM, :] = vt[h * MLA_V_DIM:(h + 1) * MLA_V_DIM, :].astype(BF16)
        vt_ref[g, r0 + MLA_V_DIM:r0 + VT_ROWS, :] = ones
    for h in range(MLA_HEADS):
        sl = slice(h * LANE, (h + 1) * LANE)
        q_ref[:, sl] = ((qa[:, sl] * cos + qb[:, sl] * sin) * q_scale).astype(BF16)
        k_ref[:, sl] = (kn[:, sl] + k_rope).astype(BF16)


def _rotate_half_cols(w):
    half = MLA_ROPE_DIM // 2
    return jnp.concatenate([-w[..., half:], w[..., :half]], axis=-1)


def _head_slab(nope, rope):
    d, h, _ = nope.shape
    pad = jnp.zeros((d, h, LANE - MLA_QK_DIM), nope.dtype)
    return jnp.concatenate([nope, rope, pad], axis=-1).reshape(d, h * LANE)


def _in_proj(x, cos128, sin128, ln_g, ln_b, w_in, q_norm_g, w_uq, kv_norm_g, w_ukv, tm, nh):
    bsz, seq, d = x.shape
    groups = MLA_HEADS // nh
    s5w = q_rank = kv_rank = d // 4
    o1, o2, o3 = s5w, s5w + q_rank, s5w + q_rank + kv_rank
    o4 = o3 + MLA_ROPE_DIM
    hl = MLA_HEADS * LANE
    w_u = w_in[:, :o1].astype(BF16)
    w_cq = w_in[:, o1:o2].astype(BF16)
    w_ckv = w_in[:, o2:o3].astype(BF16)
    w_kr = w_in[:, o3:o4]
    zl = jnp.zeros((d, MLA_NOPE_DIM), F32)
    zr = jnp.zeros((d, LANE - MLA_QK_DIM), F32)
    w_kr2 = jnp.concatenate([zl, w_kr, zr, zl, _rotate_half_cols(w_kr), zr], axis=-1).astype(BF16)
    w_g = w_in[:, o4:].astype(BF16)
    wq3 = w_uq.reshape(q_rank, MLA_HEADS, MLA_QK_DIM)
    wq_nope, wq_rope = wq3[..., :MLA_NOPE_DIM], wq3[..., MLA_NOPE_DIM:]
    w_q = _head_slab(wq_nope, wq_rope).astype(BF16)
    w_qs = _head_slab(jnp.zeros_like(wq_nope), _rotate_half_cols(wq_rope)).astype(BF16)
    wkv3 = w_ukv.reshape(kv_rank, MLA_HEADS, MLA_NOPE_DIM + MLA_V_DIM)
    w_k = _head_slab(wkv3[..., :MLA_NOPE_DIM],
                     jnp.zeros((kv_rank, MLA_HEADS, MLA_ROPE_DIM), F32)).astype(BF16)
    w_vt = wkv3[..., MLA_NOPE_DIM:].reshape(kv_rank, MLA_HEADS * MLA_V_DIM).T.astype(BF16)
    row = lambda w: pl.BlockSpec((None, tm, w), lambda b, i: (b, i, 0))
    tok = lambda w, dt: jax.ShapeDtypeStruct((bsz, seq, w), dt)
    consts = [ln_g.reshape(1, d), ln_b.reshape(1, d), w_u, w_cq, w_ckv, w_kr2, w_g,
              q_norm_g.reshape(1, q_rank), kv_norm_g.reshape(1, kv_rank), w_q, w_qs, w_k, w_vt]
    return pl.pallas_call(
        functools.partial(_in_proj_kernel, q_scale=MLA_QK_DIM ** -0.5 * math.log2(math.e), nh=nh),
        grid=(bsz, seq // tm),
        in_specs=[row(d), row(LANE), row(LANE)] + [_const_spec(c.shape) for c in consts],
        out_specs=[row(s5w), row(hl), row(hl),
                   pl.BlockSpec((None, None, groups, nh * VT_ROWS, tm), lambda b, i: (b, i, 0, 0, 0)), row(2 * d)],
        out_shape=[tok(s5w, F32), tok(hl, BF16), tok(hl, BF16),
                   jax.ShapeDtypeStruct((bsz, seq // tm, groups, nh * VT_ROWS, tm), BF16), tok(2 * d, BF16)],
        compiler_params=_params("parallel", "parallel"),
        name="in_proj",
    )(x, cos128, sin128, *consts)


def _s5_scan_kernel(u_ref, bblk_ref, are_ref, aim_ref, cblk_ref, d_ref, wglu_ref, out_ref,
                    hbuf, st_re, st_im, *, bsz, tt, nstate):
    @pl.when(pl.program_id(0) == 0)
    def _():
        st_re[...] = jnp.zeros_like(st_re)
        st_im[...] = jnp.zeros_like(st_im)

    width = u_ref.shape[-1]
    u = jnp.swapaxes(u_ref[...], 0, 1).reshape(tt * bsz, width)
    bu = _dot(u.astype(BF16), bblk_ref[...])
    nt = nstate // LANE
    for c in range(2 * nt):
        hbuf[c] = bu[:, c * LANE:(c + 1) * LANE]
    ar = [jnp.broadcast_to(are_ref[:, c * LANE:(c + 1) * LANE], (bsz, LANE)) for c in range(nt)]
    ai = [jnp.broadcast_to(aim_ref[:, c * LANE:(c + 1) * LANE], (bsz, LANE)) for c in range(nt)]

    def step(t, carry):
        hr, hi = carry
        rows = pl.ds(pl.multiple_of(t * bsz, bsz), bsz)
        nr, ni = [], []
        for c in range(nt):
            r = ar[c] * hr[c] - ai[c] * hi[c] + hbuf[c, rows, :]
            i = ar[c] * hi[c] + ai[c] * hr[c] + hbuf[nt + c, rows, :]
            hbuf[c, rows, :] = r
            hbuf[nt + c, rows, :] = i
            nr.append(r)
            ni.append(i)
        return tuple(nr), tuple(ni)

    def steps(tb, carry):
        for k in range(SCAN_UNROLL):
            carry = step(tb * SCAN_UNROLL + k, carry)
        return carry

    init = (tuple(st_re[c] for c in range(nt)), tuple(st_im[c] for c in range(nt)))
    hr, hi = lax.fori_loop(0, tt // SCAN_UNROLL, steps, init)
    for c in range(nt):
        st_re[c] = hr[c]
        st_im[c] = hi[c]
    half = tt * bsz // 2
    ys = []
    for r in range(2):
        rows = slice(r * half, (r + 1) * half)
        hs = jnp.concatenate([hbuf[c, rows, :].astype(BF16) for c in range(2 * nt)], axis=-1)
        ys.append(_dot(hs, cblk_ref[...]))
    y = jnp.concatenate(ys, axis=0) + d_ref[...] * u
    y = jnp.swapaxes(y.reshape(tt, bsz, width), 0, 1).reshape(bsz * tt, width)
    z = _dot(_gelu_exact(y).astype(BF16), wglu_ref[...])
    dm = z.shape[-1] // 2
    s_out = z[:, :dm] * jax.nn.sigmoid(z[:, dm:])
    out_ref[...] = s_out.reshape(bsz, tt, dm).astype(BF16)


def _s5_scan(u3, ab_re, ab_im, bb_re, bb_im, c_re, c_im, d_skip, w_glu, tt):
    bsz, seq, width = u3.shape
    g, p, h = bb_re.shape
    nstate = g * p
    eye = jnp.eye(g, dtype=F32)
    blk_b = lambda bb: jnp.einsum('gph,gk->ghkp', bb, eye).reshape(g * h, nstate)
    bblk = jnp.concatenate([blk_b(bb_re), blk_b(bb_im)], axis=1).astype(BF16)
    blk_c = lambda c: jnp.einsum('ghp,gk->gpkh', c, eye).reshape(nstate, g * h)
    cblk = jnp.concatenate([blk_c(c_re), -blk_c(c_im)], axis=0).astype(BF16)
    dm2 = w_glu.shape[-1]
    consts = [bblk, ab_re.reshape(1, nstate), ab_im.reshape(1, nstate), cblk,
              d_skip.reshape(1, width), w_glu.astype(BF16)]
    return pl.pallas_call(
        functools.partial(_s5_scan_kernel, bsz=bsz, tt=tt, nstate=nstate),
        grid=(seq // tt,),
        in_specs=[pl.BlockSpec((bsz, tt, width), lambda i: (0, i, 0))] + [_const_spec(c.shape) for c in consts],
        out_specs=pl.BlockSpec((bsz, tt, dm2 // 2), lambda i: (0, i, 0)),
        out_shape=jax.ShapeDtypeStruct((bsz, seq, dm2 // 2), BF16),
        scratch_shapes=[pltpu.VMEM((2 * nstate // LANE, bsz * tt, LANE), F32),
                        pltpu.VMEM((nstate // LANE, bsz, LANE), F32),
                        pltpu.VMEM((nstate // LANE, bsz, LANE), F32)],
        compiler_params=_params("arbitrary"),
        name="s5_scan",
    )(u3, *consts)


def _attn_kernel(q_ref, k_ref, vt_ref, o_ref, s_scr, p_scr, acc_scr, *, tq, nh):
    qi = pl.program_id(2)
    heads = [slice(h * LANE, (h + 1) * LANE) for h in range(nh)]
    acc_scr[...] = jnp.zeros_like(acc_scr)

    def scores(blk, h, masked):
        rows = pl.ds(pl.multiple_of(blk * tq, tq), tq)
        s = _dot_nt(k_ref[rows, heads[h]], q_ref[:, heads[h]])
        if masked:
            keep = (lax.broadcasted_iota(jnp.int32, (tq, tq), 0) <= lax.broadcasted_iota(jnp.int32, (tq, tq), 1))
            s = jnp.where(keep, s, NEG_INF)
        s_scr[h] = s

    def chunk(blk, nxt, ms, masked):
        new_ms = []
        for h in range(nh):
            if h + QK_AHEAD < nh:
                scores(blk, h + QK_AHEAD, masked)
            else:
                scores(nxt, h + QK_AHEAD - nh, False)
            tiles = [s_scr[h, r * F32_SUBLANES:(r + 1) * F32_SUBLANES, :] for r in range(tq // F32_SUBLANES)]
            part = tiles[:MAX_CHAINS]
            for r, t in enumerate(tiles[MAX_CHAINS:]):
                part[r % MAX_CHAINS] = jnp.maximum(part[r % MAX_CHAINS], t)
            m8 = functools.reduce(jnp.maximum, part)
            m_new = jnp.maximum(ms[h], jnp.max(m8, axis=0, keepdims=True))
            for r in range(tq // BF16_SUBLANES):
                sl = slice(r * BF16_SUBLANES, (r + 1) * BF16_SUBLANES)
                p_scr[h, sl, :] = jnp.exp2((s_scr[h, sl, :] - m_new).astype(BF16))
            vt = vt_ref[blk, h * VT_ROWS:(h + 1) * VT_ROWS, :]
            acc_scr[h] = jnp.exp2(ms[h] - m_new) * acc_scr[h] + _dot(vt, p_scr[h])
            new_ms.append(m_new)
        return tuple(new_ms)

    for h in range(min(QK_AHEAD, nh)):
        scores(qi, h, True)
    ms = (jnp.full((1, tq), NEG_INF, F32),) * nh
    ms = chunk(qi, 0, ms, True)
    lax.fori_loop(0, qi, lambda j, c: chunk(j, j + 1, c, False), ms)
    ot = jnp.concatenate([acc_scr[h, :MLA_V_DIM, :] / acc_scr[h, MLA_V_DIM:MLA_V_DIM + 1, :] for h in range(nh)],
                         axis=0)
    o_ref[...] = ot.T.astype(BF16)


def _attention(q, k, vt, tq, nh):
    bsz, seq, _ = q.shape
    groups = MLA_HEADS // nh
    return pl.pallas_call(
        functools.partial(_attn_kernel, tq=tq, nh=nh),
        grid=(bsz, groups, seq // tq),
        in_specs=[pl.BlockSpec((None, tq, nh * LANE), lambda b, h, i: (b, i, h)),
                  pl.BlockSpec((None, seq, nh * LANE), lambda b, h, i: (b, 0, h)),
                  pl.BlockSpec((None, seq // tq, None, nh * VT_ROWS, tq), lambda b, h, i: (b, 0, h, 0, 0))],
        out_specs=pl.BlockSpec((None, tq, nh * MLA_V_DIM), lambda b, h, i: (b, i, h)),
        out_shape=jax.ShapeDtypeStruct((bsz, seq, MLA_HEADS * MLA_V_DIM), BF16),
        scratch_shapes=[pltpu.VMEM((nh, tq, tq), F32), pltpu.VMEM((nh, tq, tq), BF16),
                        pltpu.VMEM((nh, VT_ROWS, tq), F32)],
        compiler_params=_params("parallel", "parallel", "arbitrary"),
        name="mla_attention",
    )(q, k, vt)


def _layer_tail_kernel(x_ref, ao_ref, so_ref, gate_ref, kx_ref, vx_ref, lng_ref, lnb_ref, w_oa_ref, w_o_ref,
                       ln1g_ref, ln1b_ref, w_xq_ref, w_xo_ref, ln2g_ref, ln2b_ref, w_up_ref, w_down_ref,
                       ln3g_ref, ln3b_ref, out_ref, *, x_scale, chunk):
    h0 = _layer_norm(x_ref[...], lng_ref[...], lnb_ref[...])
    d = h0.shape[-1]
    a_out = _dot(ao_ref[...], w_oa_ref[...])
    gate = gate_ref[...]
    mixed = gate[:, :d].astype(F32) * so_ref[...].astype(F32) + gate[:, d:].astype(F32) * a_out
    mix = _dot(mixed.astype(BF16), w_o_ref[...])
    h1 = _layer_norm(DN_ALPHA * h0 + mix, ln1g_ref[...], ln1b_ref[...])
    qx = (_dot(h1.astype(BF16), w_xq_ref[...]) * x_scale).astype(BF16)
    hd = d // XATTN_HEADS
    xa = jnp.zeros_like(h1)
    for hh in range(XATTN_HEADS):
        cols = slice(hh * hd, (hh + 1) * hd)
        s = _dot_nt(qx[:, cols], kx_ref[:, cols])
        p = jnp.exp(s - jnp.max(s, axis=-1, keepdims=True))
        o = _dot(p.astype(BF16), vx_ref[:, cols]) / jnp.sum(p, axis=-1, keepdims=True)
        xa = xa + _dot(o.astype(BF16), w_xo_ref[cols, :])
    h2 = _layer_norm(DN_ALPHA * h1 + xa, ln2g_ref[...], ln2b_ref[...])
    hb = h2.astype(BF16)
    ff = jnp.zeros_like(h2)
    for c in range(w_up_ref.shape[-1] // chunk):
        cols = slice(c * chunk, (c + 1) * chunk)
        a = jnp.maximum(_dot(hb, w_up_ref[:, cols]), 0.0)
        ff = ff + _dot((a * a).astype(BF16), w_down_ref[cols, :])
    out_ref[...] = _layer_norm(DN_ALPHA * h2 + ff, ln3g_ref[...], ln3b_ref[...])


def _layer_tail(x, ao, so, gate, kx, vx, ln_g, ln_b, w_oa, w_o, ln1_g, ln1_b, w_xq, w_xo, ln2_g, ln2_b,
                w_up, w_down, ln3_g, ln3_b, tm):
    bsz, seq, d = x.shape
    m = kx.shape[1]
    hidden = w_up.shape[-1]
    vec = lambda a: a.reshape(1, d)
    tok = lambda w: pl.BlockSpec((None, tm, w), lambda b, i: (b, i, 0))
    per_b = pl.BlockSpec((None, m, d), lambda b, i: (b, 0, 0))
    consts = [vec(ln_g), vec(ln_b), w_oa.astype(BF16), w_o.astype(BF16), vec(ln1_g), vec(ln1_b),
              w_xq.astype(BF16), w_xo.astype(BF16), vec(ln2_g), vec(ln2_b),
              w_up.astype(BF16), w_down.astype(BF16), vec(ln3_g), vec(ln3_b)]
    return pl.pallas_call(
        functools.partial(_layer_tail_kernel, x_scale=(d // XATTN_HEADS) ** -0.5, chunk=min(hidden, 1024)),
        grid=(bsz, seq // tm),
        in_specs=[tok(d), tok(ao.shape[-1]), tok(d), tok(2 * d), per_b, per_b] + [_const_spec(c.shape) for c in consts],
        out_specs=tok(d),
        out_shape=jax.ShapeDtypeStruct((bsz, seq, d), F32),
        compiler_params=_params("parallel", "parallel"),
        name="layer_tail",
    )(x, ao, so, gate, kx, vx, *consts)


def kernel(x, mem, positions, ln_in_g, ln_in_b, w_in, s5_lam_re, s5_lam_im, s5_log_dt, s5_b_re, s5_b_im,
           s5_c_re, s5_c_im, s5_d, w_glu, q_norm_g, w_uq, kv_norm_g, w_ukv, w_oa, w_o, ln1_g, ln1_b,
           w_xq, w_xk, w_xv, w_xo, ln2_g, ln2_b, w_up, w_down, ln3_g, ln3_b):
    bsz, seq, d = x.shape
    n = bsz * seq
    assert w_in.shape[0] == DEPTH == 1
    tm = min(seq, 512)
    tq = min(seq, 512)
    nh = 4
    tt = min(seq, 128)

    cos, sin = _rope_tables(positions)
    ones = jnp.ones((bsz, seq, MLA_NOPE_DIM), F32)
    zeros = jnp.zeros((bsz, seq, LANE - MLA_QK_DIM), F32)
    cos128 = jnp.concatenate([ones, cos, cos, zeros], axis=-1)
    sin128 = jnp.concatenate([0.0 * ones, sin, sin, zeros], axis=-1)

    assert tm == tq
    u, q, k, vt, gate = _in_proj(x, cos128, sin128, ln_in_g, ln_in_b, w_in[0],
                                 q_norm_g[0], w_uq[0], kv_norm_g[0], w_ukv[0], tm, nh)

    ab_re, ab_im, bb_re, bb_im = _s5_discretize(s5_lam_re[0], s5_lam_im[0], s5_log_dt[0], s5_b_re[0], s5_b_im[0])
    s_out = _s5_scan(u, ab_re, ab_im, bb_re, bb_im, s5_c_re[0], s5_c_im[0],
                     s5_d[0], w_glu[0], tt)

    a_o = _attention(q, k, vt, tq, nh)

    kx, vx = _mem_kv(mem, w_xk[0], w_xv[0])
    return _layer_tail(x, a_o, s_out, gate, kx, vx, ln_in_g, ln_in_b, w_oa[0], w_o[0], ln1_g[0], ln1_b[0],
                       w_xq[0], w_xo[0], ln2_g[0], ln2_b[0], w_up[0], w_down[0], ln3_g[0], ln3_b[0], tm)
```

```python
import functools
import math

import jax
import jax.numpy as jnp
from jax import lax
from jax.experimental import pallas as pl
from jax.experimental.pallas import tpu as pltpu

F32 = jnp.float32
BF16 = jnp.bfloat16

S5_GROUP_CH = 16
S5_STATE = 64
S5_MAX_RE = -1e-4
MLA_HEADS = 8
MLA_NOPE_DIM = 64
MLA_ROPE_DIM = 32
MLA_QK_DIM = MLA_NOPE_DIM + MLA_ROPE_DIM
MLA_V_DIM = 64
ROPE_THETA = 10000.0
XATTN_HEADS = 4
LN_EPS = 1e-5
RMS_EPS = 1e-6
NEG_INF = -1e30
DEPTH = 1
DN_ALPHA = (2.0 * DEPTH) ** 0.25

SCAN_UNROLL = 4
MAX_CHAINS = 4
MLP_CHUNK = 1024
QK_AHEAD = 2
REF_MARGIN = 64.0
F32_SUBLANES = 8
BF16_SUBLANES = 16
VT_ROWS = MLA_V_DIM + BF16_SUBLANES
LANE = 128
VMEM_LIMIT = 56 * 1024 * 1024


def _const_spec(shape):
    nd = len(shape)
    return pl.BlockSpec(shape, lambda *_: (0,) * nd, pipeline_mode=pl.Buffered(1))


def _params(*sem, flags=None):
    return pltpu.CompilerParams(dimension_semantics=sem, vmem_limit_bytes=VMEM_LIMIT, flags=flags)


def _layer_norm(x, g, b):
    mu = jnp.mean(x, axis=-1, keepdims=True)
    xc = x - mu
    var = jnp.mean(xc * xc, axis=-1, keepdims=True)
    return xc * lax.rsqrt(var + LN_EPS) * g + b


def _rms_norm(x, g):
    return x * lax.rsqrt(jnp.mean(x * x, axis=-1, keepdims=True) + RMS_EPS) * g


def _gelu_exact(x):
    return 0.5 * x * (1.0 + lax.erf(x * (0.5 ** 0.5)))


def _dot(a, b):
    return jnp.dot(a, b, preferred_element_type=F32)


def _dot_nt(a, b):
    return lax.dot_general(a, b, (((1,), (1,)), ((), ())), preferred_element_type=F32)


def _s5_disc_kernel(lr_ref, li_ref, ldt_ref, br_ref, bi_ref, are_ref, aim_ref, bbr_ref, bbi_ref):
    lr = jnp.minimum(lr_ref[...], S5_MAX_RE)
    li = li_ref[...]
    dt = jnp.exp(ldt_ref[...])
    mag = jnp.exp(lr * dt)
    ang = li * dt
    ab_re = mag * jnp.cos(ang)
    ab_im = mag * jnp.sin(ang)
    den = lr * lr + li * li
    nr = ab_re - 1.0
    f_re = (nr * lr + ab_im * li) / den
    f_im = (ab_im * lr - nr * li) / den
    br = br_ref[...]
    bi = bi_ref[...]
    are_ref[...] = ab_re
    aim_ref[...] = ab_im
    bbr_ref[...] = f_re * br - f_im * bi
    bbi_ref[...] = f_re * bi + f_im * br


def _s5_discretize(lam_re, lam_im, log_dt, b_re, b_im):
    g, p = lam_re.shape
    h = b_re.shape[-1]
    n = g * p
    col = lambda a: a.reshape(n, 1)
    ldt = jnp.broadcast_to(log_dt[:, None], (g, p))
    outs = pl.pallas_call(
        _s5_disc_kernel,
        out_shape=[jax.ShapeDtypeStruct((n, 1), F32), jax.ShapeDtypeStruct((n, 1), F32),
                   jax.ShapeDtypeStruct((n, h), F32), jax.ShapeDtypeStruct((n, h), F32)],
        name="s5_discretize",
    )(col(lam_re), col(lam_im), col(ldt), b_re.reshape(n, h), b_im.reshape(n, h))
    ab_re, ab_im, bb_re, bb_im = outs
    return ab_re.reshape(g, p), ab_im.reshape(g, p), bb_re.reshape(g, p, h), bb_im.reshape(g, p, h)


def _rope_kernel(pos_ref, inv_ref, cos_ref, sin_ref):
    ang = pos_ref[...].astype(F32) * inv_ref[...]
    cos_ref[...] = jnp.cos(ang)
    sin_ref[...] = jnp.sin(ang)


def _rope_tables(positions):
    bsz, seq = positions.shape
    half = MLA_ROPE_DIM // 2
    per_row = LANE // half
    rows = bsz * seq // per_row
    inv = ROPE_THETA ** (-jnp.arange(0, MLA_ROPE_DIM, 2, dtype=F32) / MLA_ROPE_DIM)
    inv_row = jnp.tile(inv, per_row).reshape(1, LANE)
    pos_rep = jnp.repeat(positions.reshape(-1), half).reshape(rows, LANE)
    tr = min(rows, 1024)
    cos, sin = pl.pallas_call(
        _rope_kernel,
        grid=(rows // tr,),
        in_specs=[pl.BlockSpec((tr, LANE), lambda i: (i, 0)), _const_spec((1, LANE))],
        out_specs=[pl.BlockSpec((tr, LANE), lambda i: (i, 0))] * 2,
        out_shape=[jax.ShapeDtypeStruct((rows, LANE), F32)] * 2,
        compiler_params=_params("parallel"),
        name="rope_table",
    )(pos_rep, inv_row)
    return cos.reshape(bsz, seq, half), sin.reshape(bsz, seq, half)


def _mem_kv_kernel(mem_ref, wk_ref, wv_ref, k_ref, v_ref):
    m = mem_ref[...].astype(BF16)
    k_ref[...] = _dot(m, wk_ref[...]).astype(BF16)
    v_ref[...] = _dot(m, wv_ref[...]).astype(BF16)


def _mem_kv(mem, w_xk, w_xv):
    bsz, m, d = mem.shape
    return pl.pallas_call(
        _mem_kv_kernel,
        grid=(bsz,),
        in_specs=[pl.BlockSpec((None, m, d), lambda b: (b, 0, 0)), _const_spec((d, d)), _const_spec((d, d))],
        out_specs=[pl.BlockSpec((None, m, d), lambda b: (b, 0, 0))] * 2,
        out_shape=[jax.ShapeDtypeStruct((bsz, m, d), BF16)] * 2,
        compiler_params=_params("parallel"),
        name="mem_kv",
    )(mem, w_xk.astype(BF16), w_xv.astype(BF16))


def _in_proj_kernel(x_ref, cos_ref, sin_ref, lng_ref, lnb_ref, w_u_ref, w_cq_ref, w_ckv_ref, w_kr_ref,
                    w_g_ref, qg_ref, kvg_ref, w_q_ref, w_qs_ref, w_k_ref, w_vt_ref,
                    u_ref, q_ref, k_ref, vt_ref, gate_ref, *, q_scale, nh):
    hb = _layer_norm(x_ref[...], lng_ref[...], lnb_ref[...]).astype(BF16)
    cos = cos_ref[...]
    sin = sin_ref[...]
    u_ref[...] = _dot(hb, w_u_ref[...])
    gate_ref[...] = jax.nn.sigmoid(_dot(hb, w_g_ref[...])).astype(BF16)
    kr2 = _dot(hb, w_kr_ref[...])
    k_rope = kr2[:, :LANE] * cos + kr2[:, LANE:] * sin
    cqn = _rms_norm(_dot(hb, w_cq_ref[...]), qg_ref[...]).astype(BF16)
    qa = _dot(cqn, w_q_ref[...])
    qb = _dot(cqn, w_qs_ref[...])
    ckvn = _rms_norm(_dot(hb, w_ckv_ref[...]), kvg_ref[...]).astype(BF16)
    kn = _dot(ckvn, w_k_ref[...])
    vt = _dot_nt(w_vt_ref[...], ckvn)
    ones = jnp.ones((VT_ROWS - MLA_V_DIM, vt.shape[-1]), BF16)
    for h in range(MLA_HEADS):
        g, r0 = h // nh, (h % nh) * VT_ROWS
        vt_ref[g, r0:r0 + MLA_V_DIM, :] = vt[h * MLA_V_DIM:(h + 1) * MLA_V_DIM, :].astype(BF16)
        vt_ref[g, r0 + MLA_V_DIM:r0 + VT_ROWS, :] = ones
    for h in range(MLA_HEADS):
        sl = slice(h * LANE, (h + 1) * LANE)
        q_ref[:, sl] = ((qa[:, sl] * cos + qb[:, sl] * sin) * q_scale).astype(BF16)
        k_ref[:, sl] = (kn[:, sl] + k_rope).astype(BF16)


def _rotate_half_cols(w):
    half = MLA_ROPE_DIM // 2
    return jnp.concatenate([-w[..., half:], w[..., :half]], axis=-1)


def _head_slab(nope, rope):
    d, h, _ = nope.shape
    pad = jnp.zeros((d, h, LANE - MLA_QK_DIM), nope.dtype)
    return jnp.concatenate([nope, rope, pad], axis=-1).reshape(d, h * LANE)


def _in_proj(x, cos128, sin128, ln_g, ln_b, w_in, q_norm_g, w_uq, kv_norm_g, w_ukv, tm, nh):
    bsz, seq, d = x.shape
    groups = MLA_HEADS // nh
    s5w = q_rank = kv_rank = d // 4
    o1, o2, o3 = s5w, s5w + q_rank, s5w + q_rank + kv_rank
    o4 = o3 + MLA_ROPE_DIM
    hl = MLA_HEADS * LANE
    w_u = w_in[:, :o1].astype(BF16)
    w_cq = w_in[:, o1:o2].astype(BF16)
    w_ckv = w_in[:, o2:o3].astype(BF16)
    w_kr = w_in[:, o3:o4]
    zl = jnp.zeros((d, MLA_NOPE_DIM), F32)
    zr = jnp.zeros((d, LANE - MLA_QK_DIM), F32)
    w_kr2 = jnp.concatenate([zl, w_kr, zr, zl, _rotate_half_cols(w_kr), zr], axis=-1).astype(BF16)
    w_g = w_in[:, o4:].astype(BF16)
    wq3 = w_uq.reshape(q_rank, MLA_HEADS, MLA_QK_DIM)
    wq_nope, wq_rope = wq3[..., :MLA_NOPE_DIM], wq3[..., MLA_NOPE_DIM:]
    w_q = _head_slab(wq_nope, wq_rope).astype(BF16)
    w_qs = _head_slab(jnp.zeros_like(wq_nope), _rotate_half_cols(wq_rope)).astype(BF16)
    wkv3 = w_ukv.reshape(kv_rank, MLA_HEADS, MLA_NOPE_DIM + MLA_V_DIM)
    w_k = _head_slab(wkv3[..., :MLA_NOPE_DIM],
                     jnp.zeros((kv_rank, MLA_HEADS, MLA_ROPE_DIM), F32)).astype(BF16)
    w_vt = wkv3[..., MLA_NOPE_DIM:].reshape(kv_rank, MLA_HEADS * MLA_V_DIM).T.astype(BF16)
    row = lambda w: pl.BlockSpec((None, tm, w), lambda b, i: (b, i, 0))
    tok = lambda w, dt: jax.ShapeDtypeStruct((bsz, seq, w), dt)
    consts = [ln_g.reshape(1, d), ln_b.reshape(1, d), w_u, w_cq, w_ckv, w_kr2, w_g,
              q_norm_g.reshape(1, q_rank), kv_norm_g.reshape(1, kv_rank), w_q, w_qs, w_k, w_vt]
    return pl.pallas_call(
        functools.partial(_in_proj_kernel, q_scale=MLA_QK_DIM ** -0.5 * math.log2(math.e), nh=nh),
        grid=(bsz, seq // tm),
        in_specs=[row(d), row(LANE), row(LANE)] + [_const_spec(c.shape) for c in consts],
        out_specs=[row(s5w), row(hl), row(hl),
                   pl.BlockSpec((None, None, groups, nh * VT_ROWS, tm), lambda b, i: (b, i, 0, 0, 0)), row(2 * d)],
        out_shape=[tok(s5w, F32), tok(hl, BF16), tok(hl, BF16),
                   jax.ShapeDtypeStruct((bsz, seq // tm, groups, nh * VT_ROWS, tm), BF16), tok(2 * d, BF16)],
        compiler_params=_params("parallel", "parallel"),
        name="in_proj",
    )(x, cos128, sin128, *consts)


def _s5_scan_kernel(u_ref, bblk_ref, are_ref, aim_ref, cblk_ref, d_ref, wglu_ref, out_ref,
                    hbuf, st_re, st_im, *, bsz, tt, nstate):
    @pl.when(pl.program_id(0) == 0)
    def _():
        st_re[...] = jnp.zeros_like(st_re)
        st_im[...] = jnp.zeros_like(st_im)

    width = u_ref.shape[-1]
    u = jnp.swapaxes(u_ref[...], 0, 1).reshape(tt * bsz, width)
    bu = _dot(u.astype(BF16), bblk_ref[...])
    nt = nstate // LANE
    for c in range(2 * nt):
        hbuf[c] = bu[:, c * LANE:(c + 1) * LANE]
    ar = [jnp.broadcast_to(are_ref[:, c * LANE:(c + 1) * LANE], (bsz, LANE)) for c in range(nt)]
    ai = [jnp.broadcast_to(aim_ref[:, c * LANE:(c + 1) * LANE], (bsz, LANE)) for c in range(nt)]

    def step(t, carry):
        hr, hi = carry
        rows = pl.ds(pl.multiple_of(t * bsz, bsz), bsz)
        nr, ni = [], []
        for c in range(nt):
            r = ar[c] * hr[c] - ai[c] * hi[c] + hbuf[c, rows, :]
            i = ar[c] * hi[c] + ai[c] * hr[c] + hbuf[nt + c, rows, :]
            hbuf[c, rows, :] = r
            hbuf[nt + c, rows, :] = i
            nr.append(r)
            ni.append(i)
        return tuple(nr), tuple(ni)

    def steps(tb, carry):
        for k in range(SCAN_UNROLL):
            carry = step(tb * SCAN_UNROLL + k, carry)
        return carry

    init = (tuple(st_re[c] for c in range(nt)), tuple(st_im[c] for c in range(nt)))
    hr, hi = lax.fori_loop(0, tt // SCAN_UNROLL, steps, init)
    for c in range(nt):
        st_re[c] = hr[c]
        st_im[c] = hi[c]
    half = tt * bsz // 2
    ys = []
    for r in range(2):
        rows = slice(r * half, (r + 1) * half)
        hs = jnp.concatenate([hbuf[c, rows, :].astype(BF16) for c in range(2 * nt)], axis=-1)
        ys.append(_dot(hs, cblk_ref[...]))
    y = jnp.concatenate(ys, axis=0) + d_ref[...] * u
    y = jnp.swapaxes(y.reshape(tt, bsz, width), 0, 1).reshape(bsz * tt, width)
    z = _dot(_gelu_exact(y).astype(BF16), wglu_ref[...])
    dm = z.shape[-1] // 2
    s_out = z[:, :dm] * jax.nn.sigmoid(z[:, dm:])
    out_ref[...] = s_out.reshape(bsz, tt, dm).astype(BF16)


def _s5_scan(u3, ab_re, ab_im, bb_re, bb_im, c_re, c_im, d_skip, w_glu, tt):
    bsz, seq, width = u3.shape
    g, p, h = bb_re.shape
    nstate = g * p
    eye = jnp.eye(g, dtype=F32)
    blk_b = lambda bb: jnp.einsum('gph,gk->ghkp', bb, eye).reshape(g * h, nstate)
    bblk = jnp.concatenate([blk_b(bb_re), blk_b(bb_im)], axis=1).astype(BF16)
    blk_c = lambda c: jnp.einsum('ghp,gk->gpkh', c, eye).reshape(nstate, g * h)
    cblk = jnp.concatenate([blk_c(c_re), -blk_c(c_im)], axis=0).astype(BF16)
    dm2 = w_glu.shape[-1]
    consts = [bblk, ab_re.reshape(1, nstate), ab_im.reshape(1, nstate), cblk,
              d_skip.reshape(1, width), w_glu.astype(BF16)]
    return pl.pallas_call(
        functools.partial(_s5_scan_kernel, bsz=bsz, tt=tt, nstate=nstate),
        grid=(seq // tt,),
        in_specs=[pl.BlockSpec((bsz, tt, width), lambda i: (0, i, 0))] + [_const_spec(c.shape) for c in consts],
        out_specs=pl.BlockSpec((bsz, tt, dm2 // 2), lambda i: (0, i, 0)),
        out_shape=jax.ShapeDtypeStruct((bsz, seq, dm2 // 2), BF16),
        scratch_shapes=[pltpu.VMEM((2 * nstate // LANE, bsz * tt, LANE), F32),
                        pltpu.VMEM((nstate // LANE, bsz, LANE), F32),
                        pltpu.VMEM((nstate // LANE, bsz, LANE), F32)],
        compiler_params=_params("arbitrary"),
        name="s5_scan",
    )(u3, *consts)


def _attn_kernel(q_ref, k_ref, vt_ref, o_ref, s_scr, p_scr, acc_scr, acc_new_scr, *, tq, nh):
    qi = pl.program_id(2)
    heads = [slice(h * LANE, (h + 1) * LANE) for h in range(nh)]
    acc_scr[...] = jnp.zeros_like(acc_scr)

    def scores(blk, h, masked=False):
        rows = pl.ds(pl.multiple_of(blk * tq, tq), tq)
        s = _dot_nt(k_ref[rows, heads[h]], q_ref[:, heads[h]])
        if masked:
            keep = (lax.broadcasted_iota(jnp.int32, (tq, tq), 0) <= lax.broadcasted_iota(jnp.int32, (tq, tq), 1))
            s = jnp.where(keep, s, NEG_INF)
        return s

    def values(blk, h):
        return vt_ref[blk, h * VT_ROWS:(h + 1) * VT_ROWS, :]

    def two_pass_chunk(blk, ms, masked):
        for h in range(min(QK_AHEAD, nh)):
            s_scr[h] = scores(blk, h, masked)
        new_ms = []
        for h in range(nh):
            if h + QK_AHEAD < nh:
                s_scr[h + QK_AHEAD] = scores(blk, h + QK_AHEAD, masked)
            tiles = [s_scr[h, r * F32_SUBLANES:(r + 1) * F32_SUBLANES, :] for r in range(tq // F32_SUBLANES)]
            part = tiles[:MAX_CHAINS]
            for r, t in enumerate(tiles[MAX_CHAINS:]):
                part[r % MAX_CHAINS] = jnp.maximum(part[r % MAX_CHAINS], t)
            m8 = functools.reduce(jnp.maximum, part)
            m_new = jnp.maximum(ms[h], jnp.max(m8, axis=0, keepdims=True))
            for r in range(tq // BF16_SUBLANES):
                sl = slice(r * BF16_SUBLANES, (r + 1) * BF16_SUBLANES)
                p_scr[h, sl, :] = jnp.exp2(s_scr[h, sl, :] - m_new).astype(BF16)
            acc_scr[h] = jnp.exp2(ms[h] - m_new) * acc_scr[h] + _dot(values(blk, h), p_scr[h])
            new_ms.append(m_new)
        return tuple(new_ms)

    def single_pass_chunk(blk, ms):
        excess = []

        def softmax(h, s):
            p_scr[h] = jnp.exp2(s - ms[h]).astype(BF16)
            excess.append(jnp.max(s, axis=0, keepdims=True) - ms[h])

        def accumulate(h):
            acc_new_scr[h] = acc_scr[h] + _dot(values(blk, h), p_scr[h])

        softmax(0, scores(blk, 0))
        for h in range(1, nh):
            softmax(h, scores(blk, h))
            accumulate(h - 1)
        accumulate(nh - 1)
        return jnp.max(functools.reduce(jnp.maximum, excess))

    def step(j, ms):
        worst = single_pass_chunk(j, ms)

        def commit(ms):
            acc_scr[...] = acc_new_scr[...]
            return ms

        return lax.cond(worst > REF_MARGIN, lambda ms: two_pass_chunk(j, ms, False), commit, ms)

    ms = (jnp.full((1, tq), NEG_INF, F32),) * nh
    ms = two_pass_chunk(qi, ms, True)
    lax.fori_loop(0, qi, step, ms)
    ot = jnp.concatenate([acc_scr[h, :MLA_V_DIM, :] / acc_scr[h, MLA_V_DIM:MLA_V_DIM + 1, :] for h in range(nh)],
                         axis=0)
    o_ref[...] = ot.T.astype(BF16)


def _attention(q, k, vt, tq, nh):
    bsz, seq, _ = q.shape
    groups = MLA_HEADS // nh
    return pl.pallas_call(
        functools.partial(_attn_kernel, tq=tq, nh=nh),
        grid=(bsz, groups, seq // tq),
        in_specs=[pl.BlockSpec((None, tq, nh * LANE), lambda b, h, i: (b, i, h)),
                  pl.BlockSpec((None, seq, nh * LANE), lambda b, h, i: (b, 0, h)),
                  pl.BlockSpec((None, seq // tq, None, nh * VT_ROWS, tq), lambda b, h, i: (b, 0, h, 0, 0))],
        out_specs=pl.BlockSpec((None, tq, nh * MLA_V_DIM), lambda b, h, i: (b, i, h)),
        out_shape=jax.ShapeDtypeStruct((bsz, seq, MLA_HEADS * MLA_V_DIM), BF16),
        scratch_shapes=[pltpu.VMEM((nh, tq, tq), F32), pltpu.VMEM((nh, tq, tq), BF16),
                        pltpu.VMEM((nh, VT_ROWS, tq), F32), pltpu.VMEM((nh, VT_ROWS, tq), F32)],
        compiler_params=_params("parallel", "parallel", "arbitrary"),
        name="mla_attention",
    )(q, k, vt)


def _layer_tail_kernel(x_ref, ao_ref, so_ref, gate_ref, kx_ref, vx_ref, lng_ref, lnb_ref, w_oa_ref, w_o_ref,
                       ln1g_ref, ln1b_ref, w_xq_ref, w_xo_ref, ln2g_ref, ln2b_ref, w_up_ref, w_down_ref,
                       ln3g_ref, ln3b_ref, out_ref, h2_scr, *, x_scale, chunk):
    @pl.when(pl.program_id(0) == 0)
    def _():
        h2_scr[...] = jnp.zeros_like(h2_scr)

    h2_prev = h2_scr[...]
    hb_prev = h2_prev.astype(BF16)
    nchunk = w_up_ref.shape[-1] // chunk

    def mlp_units():
        ff = jnp.zeros_like(h2_prev)
        for c in range(nchunk):
            cols = slice(c * chunk, (c + 1) * chunk)
            a = jnp.maximum(_dot(hb_prev, w_up_ref[:, cols]), 0.0)
            a = (a * a).astype(BF16)
            yield None
            ff = ff + _dot(a, w_down_ref[cols, :])
            yield ff

    units = mlp_units()
    ff = [None]

    def issue(n):
        for _ in range(n):
            ff[0] = next(units, ff[0])

    h0 = _layer_norm(x_ref[...], lng_ref[...], lnb_ref[...])
    d = h0.shape[-1]
    a_out = _dot(ao_ref[...], w_oa_ref[...])
    issue(1)
    gate = gate_ref[...]
    mixed = gate[:, :d].astype(F32) * so_ref[...].astype(F32) + gate[:, d:].astype(F32) * a_out
    mix = _dot(mixed.astype(BF16), w_o_ref[...])
    issue(2)
    h1 = _layer_norm(DN_ALPHA * h0 + mix, ln1g_ref[...], ln1b_ref[...])
    qx = (_dot(h1.astype(BF16), w_xq_ref[...]) * x_scale).astype(BF16)
    issue(2)
    hd = d // XATTN_HEADS
    xa = jnp.zeros_like(h1)
    for hh in range(XATTN_HEADS):
        cols = slice(hh * hd, (hh + 1) * hd)
        s = _dot_nt(qx[:, cols], kx_ref[:, cols])
        issue(1)
        p = jnp.exp(s - jnp.max(s, axis=-1, keepdims=True))
        o = _dot(p.astype(BF16), vx_ref[:, cols]) / jnp.sum(p, axis=-1, keepdims=True)
        xa = xa + _dot(o.astype(BF16), w_xo_ref[cols, :])
    issue(2 * nchunk)
    h2 = _layer_norm(DN_ALPHA * h1 + xa, ln2g_ref[...], ln2b_ref[...])
    out_ref[...] = _layer_norm(DN_ALPHA * h2_prev + ff[0], ln3g_ref[...], ln3b_ref[...])
    h2_scr[...] = h2


def _layer_tail(x, ao, so, gate, kx, vx, ln_g, ln_b, w_oa, w_o, ln1_g, ln1_b, w_xq, w_xo, ln2_g, ln2_b,
                w_up, w_down, ln3_g, ln3_b, tm):
    bsz, seq, d = x.shape
    m = kx.shape[1]
    hidden = w_up.shape[-1]
    per_seq = seq // tm
    ntile = bsz * per_seq
    vec = lambda a: a.reshape(1, d)
    flat = lambda a: a.reshape(bsz * seq, a.shape[-1])
    cur = lambda i: jnp.minimum(i, ntile - 1)
    tok = lambda w: pl.BlockSpec((tm, w), lambda i: (cur(i), 0))
    per_b = pl.BlockSpec((None, m, d), lambda i: (cur(i) // per_seq, 0, 0))
    consts = [vec(ln_g), vec(ln_b), w_oa.astype(BF16), w_o.astype(BF16), vec(ln1_g), vec(ln1_b),
              w_xq.astype(BF16), w_xo.astype(BF16), vec(ln2_g), vec(ln2_b),
              w_up.astype(BF16), w_down.astype(BF16), vec(ln3_g), vec(ln3_b)]
    out = pl.pallas_call(
        functools.partial(_layer_tail_kernel, x_scale=(d // XATTN_HEADS) ** -0.5, chunk=min(hidden, MLP_CHUNK)),
        grid=(ntile + 1,),
        in_specs=[tok(d), tok(ao.shape[-1]), tok(d), tok(2 * d), per_b, per_b] + [_const_spec(c.shape) for c in consts],
        out_specs=pl.BlockSpec((tm, d), lambda i: (jnp.maximum(i - 1, 0), 0)),
        out_shape=jax.ShapeDtypeStruct((bsz * seq, d), F32),
        scratch_shapes=[pltpu.VMEM((tm, d), F32)],
        compiler_params=_params("arbitrary"),
        name="layer_tail",
    )(flat(x), flat(ao), flat(so), flat(gate), kx, vx, *consts)
    return out.reshape(bsz, seq, d)


def kernel(x, mem, positions, ln_in_g, ln_in_b, w_in, s5_lam_re, s5_lam_im, s5_log_dt, s5_b_re, s5_b_im,
           s5_c_re, s5_c_im, s5_d, w_glu, q_norm_g, w_uq, kv_norm_g, w_ukv, w_oa, w_o, ln1_g, ln1_b,
           w_xq, w_xk, w_xv, w_xo, ln2_g, ln2_b, w_up, w_down, ln3_g, ln3_b):
    bsz, seq, d = x.shape
    n = bsz * seq
    assert w_in.shape[0] == DEPTH == 1
    tm = min(seq, 512)
    tq = min(seq, 512)
    nh = 4
    tt = min(seq, 128)

    cos, sin = _rope_tables(positions)
    ones = jnp.ones((bsz, seq, MLA_NOPE_DIM), F32)
    zeros = jnp.zeros((bsz, seq, LANE - MLA_QK_DIM), F32)
    cos128 = jnp.concatenate([ones, cos, cos, zeros], axis=-1)
    sin128 = jnp.concatenate([0.0 * ones, sin, sin, zeros], axis=-1)

    assert tm == tq
    u, q, k, vt, gate = _in_proj(x, cos128, sin128, ln_in_g, ln_in_b, w_in[0],
                                 q_norm_g[0], w_uq[0], kv_norm_g[0], w_ukv[0], tm, nh)

    ab_re, ab_im, bb_re, bb_im = _s5_discretize(s5_lam_re[0], s5_lam_im[0], s5_log_dt[0], s5_b_re[0], s5_b_im[0])
    s_out = _s5_scan(u, ab_re, ab_im, bb_re, bb_im, s5_c_re[0], s5_c_im[0],
                     s5_d[0], w_glu[0], tt)

    a_o = _attention(q, k, vt, tq, nh)

    kx, vx = _mem_kv(mem, w_xk[0], w_xv[0])
    return _layer_tail(x, a_o, s_out, gate, kx, vx, ln_in_g, ln_in_b, w_oa[0], w_o[0], ln1_g[0], ln1_b[0],
                       w_xq[0], w_xo[0], ln2_g[0], ln2_b[0], w_up[0], w_down[0], ln3_g[0], ln3_b[0], tm)
```

```python
import functools
import math

import jax
import jax.numpy as jnp
from jax import lax
from jax.experimental import pallas as pl
from jax.experimental.pallas import tpu as pltpu

F32 = jnp.float32
BF16 = jnp.bfloat16

S5_GROUP_CH = 16
S5_STATE = 64
S5_MAX_RE = -1e-4
MLA_HEADS = 8
MLA_NOPE_DIM = 64
MLA_ROPE_DIM = 32
MLA_QK_DIM = MLA_NOPE_DIM + MLA_ROPE_DIM
MLA_V_DIM = 64
ROPE_THETA = 10000.0
XATTN_HEADS = 4
LN_EPS = 1e-5
RMS_EPS = 1e-6
NEG_INF = -1e30
DEPTH = 1
DN_ALPHA = (2.0 * DEPTH) ** 0.25

SCAN_UNROLL = 4
MAX_CHAINS = 4
MLP_CHUNK = 1024
QK_AHEAD = 2
FAST_CHUNKS = 2
REF_MARGIN = 64.0
F32_SUBLANES = 8
BF16_SUBLANES = 16
VT_ROWS = MLA_V_DIM + BF16_SUBLANES
LANE = 128
VMEM_LIMIT = 56 * 1024 * 1024


def _const_spec(shape):
    nd = len(shape)
    return pl.BlockSpec(shape, lambda *_: (0,) * nd, pipeline_mode=pl.Buffered(1))


def _params(*sem, flags=None):
    return pltpu.CompilerParams(dimension_semantics=sem, vmem_limit_bytes=VMEM_LIMIT, flags=flags)


def _layer_norm(x, g, b):
    mu = jnp.mean(x, axis=-1, keepdims=True)
    xc = x - mu
    var = jnp.mean(xc * xc, axis=-1, keepdims=True)
    return xc * lax.rsqrt(var + LN_EPS) * g + b


def _rms_norm(x, g):
    return x * lax.rsqrt(jnp.mean(x * x, axis=-1, keepdims=True) + RMS_EPS) * g


def _gelu_exact(x):
    return 0.5 * x * (1.0 + lax.erf(x * (0.5 ** 0.5)))


def _dot(a, b):
    return jnp.dot(a, b, preferred_element_type=F32)


def _dot_nt(a, b):
    return lax.dot_general(a, b, (((1,), (1,)), ((), ())), preferred_element_type=F32)


def _s5_disc_kernel(lr_ref, li_ref, ldt_ref, br_ref, bi_ref, are_ref, aim_ref, bbr_ref, bbi_ref):
    lr = jnp.minimum(lr_ref[...], S5_MAX_RE)
    li = li_ref[...]
    dt = jnp.exp(ldt_ref[...])
    mag = jnp.exp(lr * dt)
    ang = li * dt
    ab_re = mag * jnp.cos(ang)
    ab_im = mag * jnp.sin(ang)
    den = lr * lr + li * li
    nr = ab_re - 1.0
    f_re = (nr * lr + ab_im * li) / den
    f_im = (ab_im * lr - nr * li) / den
    br = br_ref[...]
    bi = bi_ref[...]
    are_ref[...] = ab_re
    aim_ref[...] = ab_im
    bbr_ref[...] = f_re * br - f_im * bi
    bbi_ref[...] = f_re * bi + f_im * br


def _s5_discretize(lam_re, lam_im, log_dt, b_re, b_im):
    g, p = lam_re.shape
    h = b_re.shape[-1]
    n = g * p
    col = lambda a: a.reshape(n, 1)
    ldt = jnp.broadcast_to(log_dt[:, None], (g, p))
    outs = pl.pallas_call(
        _s5_disc_kernel,
        out_shape=[jax.ShapeDtypeStruct((n, 1), F32), jax.ShapeDtypeStruct((n, 1), F32),
                   jax.ShapeDtypeStruct((n, h), F32), jax.ShapeDtypeStruct((n, h), F32)],
        name="s5_discretize",
    )(col(lam_re), col(lam_im), col(ldt), b_re.reshape(n, h), b_im.reshape(n, h))
    ab_re, ab_im, bb_re, bb_im = outs
    return ab_re.reshape(g, p), ab_im.reshape(g, p), bb_re.reshape(g, p, h), bb_im.reshape(g, p, h)


def _rope_kernel(pos_ref, inv_ref, cos_ref, sin_ref):
    ang = pos_ref[...].astype(F32) * inv_ref[...]
    cos_ref[...] = jnp.cos(ang)
    sin_ref[...] = jnp.sin(ang)


def _rope_tables(positions):
    bsz, seq = positions.shape
    half = MLA_ROPE_DIM // 2
    per_row = LANE // half
    rows = bsz * seq // per_row
    inv = ROPE_THETA ** (-jnp.arange(0, MLA_ROPE_DIM, 2, dtype=F32) / MLA_ROPE_DIM)
    inv_row = jnp.tile(inv, per_row).reshape(1, LANE)
    pos_rep = jnp.repeat(positions.reshape(-1), half).reshape(rows, LANE)
    tr = min(rows, 1024)
    cos, sin = pl.pallas_call(
        _rope_kernel,
        grid=(rows // tr,),
        in_specs=[pl.BlockSpec((tr, LANE), lambda i: (i, 0)), _const_spec((1, LANE))],
        out_specs=[pl.BlockSpec((tr, LANE), lambda i: (i, 0))] * 2,
        out_shape=[jax.ShapeDtypeStruct((rows, LANE), F32)] * 2,
        compiler_params=_params("parallel"),
        name="rope_table",
    )(pos_rep, inv_row)
    return cos.reshape(bsz, seq, half), sin.reshape(bsz, seq, half)


def _mem_kv_kernel(mem_ref, wk_ref, wv_ref, k_ref, v_ref):
    m = mem_ref[...].astype(BF16)
    k_ref[...] = _dot(m, wk_ref[...]).astype(BF16)
    v_ref[...] = _dot(m, wv_ref[...]).astype(BF16)


def _mem_kv(mem, w_xk, w_xv):
    bsz, m, d = mem.shape
    return pl.pallas_call(
        _mem_kv_kernel,
        grid=(bsz,),
        in_specs=[pl.BlockSpec((None, m, d), lambda b: (b, 0, 0)), _const_spec((d, d)), _const_spec((d, d))],
        out_specs=[pl.BlockSpec((None, m, d), lambda b: (b, 0, 0))] * 2,
        out_shape=[jax.ShapeDtypeStruct((bsz, m, d), BF16)] * 2,
        compiler_params=_params("parallel"),
        name="mem_kv",
    )(mem, w_xk.astype(BF16), w_xv.astype(BF16))


def _in_proj_kernel(x_ref, cos_ref, sin_ref, lng_ref, lnb_ref, w_u_ref, w_cq_ref, w_ckv_ref, w_kr_ref,
                    w_g_ref, qg_ref, kvg_ref, w_q_ref, w_qs_ref, w_k_ref, w_vt_ref,
                    u_ref, q_ref, k_ref, vt_ref, gate_ref, *, q_scale, nh):
    hb = _layer_norm(x_ref[...], lng_ref[...], lnb_ref[...]).astype(BF16)
    cos = cos_ref[...]
    sin = sin_ref[...]
    u_ref[...] = _dot(hb, w_u_ref[...])
    gate_ref[...] = jax.nn.sigmoid(_dot(hb, w_g_ref[...])).astype(BF16)
    kr2 = _dot(hb, w_kr_ref[...])
    k_rope = kr2[:, :LANE] * cos + kr2[:, LANE:] * sin
    cqn = _rms_norm(_dot(hb, w_cq_ref[...]), qg_ref[...]).astype(BF16)
    qa = _dot(cqn, w_q_ref[...])
    qb = _dot(cqn, w_qs_ref[...])
    ckvn = _rms_norm(_dot(hb, w_ckv_ref[...]), kvg_ref[...]).astype(BF16)
    kn = _dot(ckvn, w_k_ref[...])
    vt = _dot_nt(w_vt_ref[...], ckvn)
    ones = jnp.ones((VT_ROWS - MLA_V_DIM, vt.shape[-1]), BF16)
    for h in range(MLA_HEADS):
        g, r0 = h // nh, (h % nh) * VT_ROWS
        vt_ref[g, r0:r0 + MLA_V_DIM, :] = vt[h * MLA_V_DIM:(h + 1) * MLA_V_DIM, :].astype(BF16)
        vt_ref[g, r0 + MLA_V_DIM:r0 + VT_ROWS, :] = ones
    for h in range(MLA_HEADS):
        sl = slice(h * LANE, (h + 1) * LANE)
        q_ref[:, sl] = ((qa[:, sl] * cos + qb[:, sl] * sin) * q_scale).astype(BF16)
        k_ref[:, sl] = (kn[:, sl] + k_rope).astype(BF16)


def _rotate_half_cols(w):
    half = MLA_ROPE_DIM // 2
    return jnp.concatenate([-w[..., half:], w[..., :half]], axis=-1)


def _head_slab(nope, rope):
    d, h, _ = nope.shape
    pad = jnp.zeros((d, h, LANE - MLA_QK_DIM), nope.dtype)
    return jnp.concatenate([nope, rope, pad], axis=-1).reshape(d, h * LANE)


def _in_proj(x, cos128, sin128, ln_g, ln_b, w_in, q_norm_g, w_uq, kv_norm_g, w_ukv, tm, nh):
    bsz, seq, d = x.shape
    groups = MLA_HEADS // nh
    s5w = q_rank = kv_rank = d // 4
    o1, o2, o3 = s5w, s5w + q_rank, s5w + q_rank + kv_rank
    o4 = o3 + MLA_ROPE_DIM
    hl = MLA_HEADS * LANE
    w_u = w_in[:, :o1].astype(BF16)
    w_cq = w_in[:, o1:o2].astype(BF16)
    w_ckv = w_in[:, o2:o3].astype(BF16)
    w_kr = w_in[:, o3:o4]
    zl = jnp.zeros((d, MLA_NOPE_DIM), F32)
    zr = jnp.zeros((d, LANE - MLA_QK_DIM), F32)
    w_kr2 = jnp.concatenate([zl, w_kr, zr, zl, _rotate_half_cols(w_kr), zr], axis=-1).astype(BF16)
    w_g = w_in[:, o4:].astype(BF16)
    wq3 = w_uq.reshape(q_rank, MLA_HEADS, MLA_QK_DIM)
    wq_nope, wq_rope = wq3[..., :MLA_NOPE_DIM], wq3[..., MLA_NOPE_DIM:]
    w_q = _head_slab(wq_nope, wq_rope).astype(BF16)
    w_qs = _head_slab(jnp.zeros_like(wq_nope), _rotate_half_cols(wq_rope)).astype(BF16)
    wkv3 = w_ukv.reshape(kv_rank, MLA_HEADS, MLA_NOPE_DIM + MLA_V_DIM)
    w_k = _head_slab(wkv3[..., :MLA_NOPE_DIM],
                     jnp.zeros((kv_rank, MLA_HEADS, MLA_ROPE_DIM), F32)).astype(BF16)
    w_vt = wkv3[..., MLA_NOPE_DIM:].reshape(kv_rank, MLA_HEADS * MLA_V_DIM).T.astype(BF16)
    row = lambda w: pl.BlockSpec((None, tm, w), lambda b, i: (b, i, 0))
    tok = lambda w, dt: jax.ShapeDtypeStruct((bsz, seq, w), dt)
    consts = [ln_g.reshape(1, d), ln_b.reshape(1, d), w_u, w_cq, w_ckv, w_kr2, w_g,
              q_norm_g.reshape(1, q_rank), kv_norm_g.reshape(1, kv_rank), w_q, w_qs, w_k, w_vt]
    return pl.pallas_call(
        functools.partial(_in_proj_kernel, q_scale=MLA_QK_DIM ** -0.5 * math.log2(math.e), nh=nh),
        grid=(bsz, seq // tm),
        in_specs=[row(d), row(LANE), row(LANE)] + [_const_spec(c.shape) for c in consts],
        out_specs=[row(s5w), row(hl), row(hl),
                   pl.BlockSpec((None, None, groups, nh * VT_ROWS, tm), lambda b, i: (b, i, 0, 0, 0)), row(2 * d)],
        out_shape=[tok(s5w, F32), tok(hl, BF16), tok(hl, BF16),
                   jax.ShapeDtypeStruct((bsz, seq // tm, groups, nh * VT_ROWS, tm), BF16), tok(2 * d, BF16)],
        compiler_params=_params("parallel", "parallel"),
        name="in_proj",
    )(x, cos128, sin128, *consts)


def _s5_scan_kernel(u_ref, bblk_ref, are_ref, aim_ref, cblk_ref, d_ref, wglu_ref, out_ref,
                    hbuf, st_re, st_im, *, bsz, tt, nstate):
    @pl.when(pl.program_id(0) == 0)
    def _():
        st_re[...] = jnp.zeros_like(st_re)
        st_im[...] = jnp.zeros_like(st_im)

    width = u_ref.shape[-1]
    u = jnp.swapaxes(u_ref[...], 0, 1).reshape(tt * bsz, width)
    bu = _dot(u.astype(BF16), bblk_ref[...])
    nt = nstate // LANE
    for c in range(2 * nt):
        hbuf[c] = bu[:, c * LANE:(c + 1) * LANE]
    ar = [jnp.broadcast_to(are_ref[:, c * LANE:(c + 1) * LANE], (bsz, LANE)) for c in range(nt)]
    ai = [jnp.broadcast_to(aim_ref[:, c * LANE:(c + 1) * LANE], (bsz, LANE)) for c in range(nt)]

    def step(t, carry):
        hr, hi = carry
        rows = pl.ds(pl.multiple_of(t * bsz, bsz), bsz)
        nr, ni = [], []
        for c in range(nt):
            r = ar[c] * hr[c] - ai[c] * hi[c] + hbuf[c, rows, :]
            i = ar[c] * hi[c] + ai[c] * hr[c] + hbuf[nt + c, rows, :]
            hbuf[c, rows, :] = r
            hbuf[nt + c, rows, :] = i
            nr.append(r)
            ni.append(i)
        return tuple(nr), tuple(ni)

    def steps(tb, carry):
        for k in range(SCAN_UNROLL):
            carry = step(tb * SCAN_UNROLL + k, carry)
        return carry

    init = (tuple(st_re[c] for c in range(nt)), tuple(st_im[c] for c in range(nt)))
    hr, hi = lax.fori_loop(0, tt // SCAN_UNROLL, steps, init)
    for c in range(nt):
        st_re[c] = hr[c]
        st_im[c] = hi[c]
    half = tt * bsz // 2
    ys = []
    for r in range(2):
        rows = slice(r * half, (r + 1) * half)
        hs = jnp.concatenate([hbuf[c, rows, :].astype(BF16) for c in range(2 * nt)], axis=-1)
        ys.append(_dot(hs, cblk_ref[...]))
    y = jnp.concatenate(ys, axis=0) + d_ref[...] * u
    y = jnp.swapaxes(y.reshape(tt, bsz, width), 0, 1).reshape(bsz * tt, width)
    z = _dot(_gelu_exact(y).astype(BF16), wglu_ref[...])
    dm = z.shape[-1] // 2
    s_out = z[:, :dm] * jax.nn.sigmoid(z[:, dm:])
    out_ref[...] = s_out.reshape(bsz, tt, dm).astype(BF16)


def _s5_scan(u3, ab_re, ab_im, bb_re, bb_im, c_re, c_im, d_skip, w_glu, tt):
    bsz, seq, width = u3.shape
    g, p, h = bb_re.shape
    nstate = g * p
    eye = jnp.eye(g, dtype=F32)
    blk_b = lambda bb: jnp.einsum('gph,gk->ghkp', bb, eye).reshape(g * h, nstate)
    bblk = jnp.concatenate([blk_b(bb_re), blk_b(bb_im)], axis=1).astype(BF16)
    blk_c = lambda c: jnp.einsum('ghp,gk->gpkh', c, eye).reshape(nstate, g * h)
    cblk = jnp.concatenate([blk_c(c_re), -blk_c(c_im)], axis=0).astype(BF16)
    dm2 = w_glu.shape[-1]
    consts = [bblk, ab_re.reshape(1, nstate), ab_im.reshape(1, nstate), cblk,
              d_skip.reshape(1, width), w_glu.astype(BF16)]
    return pl.pallas_call(
        functools.partial(_s5_scan_kernel, bsz=bsz, tt=tt, nstate=nstate),
        grid=(seq // tt,),
        in_specs=[pl.BlockSpec((bsz, tt, width), lambda i: (0, i, 0))] + [_const_spec(c.shape) for c in consts],
        out_specs=pl.BlockSpec((bsz, tt, dm2 // 2), lambda i: (0, i, 0)),
        out_shape=jax.ShapeDtypeStruct((bsz, seq, dm2 // 2), BF16),
        scratch_shapes=[pltpu.VMEM((2 * nstate // LANE, bsz * tt, LANE), F32),
                        pltpu.VMEM((nstate // LANE, bsz, LANE), F32),
                        pltpu.VMEM((nstate // LANE, bsz, LANE), F32)],
        compiler_params=_params("arbitrary"),
        name="s5_scan",
    )(u3, *consts)


def _attn_kernel(q_ref, k_ref, vt_ref, o_ref, s_scr, p_scr, acc_scr, acc_new_scr, *, tq, nh):
    qi = pl.program_id(2)
    heads = [slice(h * LANE, (h + 1) * LANE) for h in range(nh)]
    acc_scr[...] = jnp.zeros_like(acc_scr)

    def scores(blk, h, masked=False, nblk=1):
        rows = pl.ds(pl.multiple_of(blk * tq, tq), nblk * tq)
        s = _dot_nt(k_ref[rows, heads[h]], q_ref[:, heads[h]])
        if masked:
            keep = (lax.broadcasted_iota(jnp.int32, (tq, tq), 0) <= lax.broadcasted_iota(jnp.int32, (tq, tq), 1))
            s = jnp.where(keep, s, NEG_INF)
        return s

    def values(blk, h):
        return vt_ref[blk, h * VT_ROWS:(h + 1) * VT_ROWS, :]

    def two_pass_chunk(blk, ms, masked):
        for h in range(min(QK_AHEAD, nh)):
            s_scr[h] = scores(blk, h, masked)
        new_ms = []
        for h in range(nh):
            if h + QK_AHEAD < nh:
                s_scr[h + QK_AHEAD] = scores(blk, h + QK_AHEAD, masked)
            tiles = [s_scr[h, r * F32_SUBLANES:(r + 1) * F32_SUBLANES, :] for r in range(tq // F32_SUBLANES)]
            part = tiles[:MAX_CHAINS]
            for r, t in enumerate(tiles[MAX_CHAINS:]):
                part[r % MAX_CHAINS] = jnp.maximum(part[r % MAX_CHAINS], t)
            m8 = functools.reduce(jnp.maximum, part)
            m_new = jnp.maximum(ms[h], jnp.max(m8, axis=0, keepdims=True))
            for r in range(tq // BF16_SUBLANES):
                sl = slice(r * BF16_SUBLANES, (r + 1) * BF16_SUBLANES)
                p_scr[h, sl, :] = jnp.exp2(s_scr[h, sl, :] - m_new).astype(BF16)
            acc_scr[h] = jnp.exp2(ms[h] - m_new) * acc_scr[h] + _dot(values(blk, h), p_scr[h, :tq, :])
            new_ms.append(m_new)
        return tuple(new_ms)

    def single_pass(blk, nblk, ms):
        excess = []

        def softmax(h):
            s = scores(blk, h, nblk=nblk)
            p_scr[h, :nblk * tq, :] = jnp.exp2(s - ms[h]).astype(BF16)
            excess.append(jnp.max(s, axis=0, keepdims=True) - ms[h])

        def accumulate(h):
            acc = acc_scr[h]
            for i in range(nblk):
                acc = acc + _dot(values(blk + i, h), p_scr[h, i * tq:(i + 1) * tq, :])
            acc_new_scr[h] = acc

        softmax(0)
        for h in range(1, nh):
            softmax(h)
            accumulate(h - 1)
        accumulate(nh - 1)
        return jnp.max(functools.reduce(jnp.maximum, excess))

    def step(blk, nblk, ms):
        worst = single_pass(blk, nblk, ms)

        def redo(ms):
            for i in range(nblk):
                ms = two_pass_chunk(blk + i, ms, False)
            return ms

        def commit(ms):
            acc_scr[...] = acc_new_scr[...]
            return ms

        return lax.cond(worst > REF_MARGIN, redo, commit, ms)

    ms = (jnp.full((1, tq), NEG_INF, F32),) * nh
    ms = two_pass_chunk(qi, ms, True)
    nwide = qi // FAST_CHUNKS
    ms = lax.fori_loop(0, nwide, lambda t, c: step(t * FAST_CHUNKS, FAST_CHUNKS, c), ms)
    lax.fori_loop(nwide * FAST_CHUNKS, qi, lambda j, c: step(j, 1, c), ms)
    ot = jnp.concatenate([acc_scr[h, :MLA_V_DIM, :] / acc_scr[h, MLA_V_DIM:MLA_V_DIM + 1, :] for h in range(nh)],
                         axis=0)
    o_ref[...] = ot.T.astype(BF16)


def _attention(q, k, vt, tq, nh):
    bsz, seq, _ = q.shape
    groups = MLA_HEADS // nh
    return pl.pallas_call(
        functools.partial(_attn_kernel, tq=tq, nh=nh),
        grid=(bsz, groups, seq // tq),
        in_specs=[pl.BlockSpec((None, tq, nh * LANE), lambda b, h, i: (b, i, h)),
                  pl.BlockSpec((None, seq, nh * LANE), lambda b, h, i: (b, 0, h)),
                  pl.BlockSpec((None, seq // tq, None, nh * VT_ROWS, tq), lambda b, h, i: (b, 0, h, 0, 0))],
        out_specs=pl.BlockSpec((None, tq, nh * MLA_V_DIM), lambda b, h, i: (b, i, h)),
        out_shape=jax.ShapeDtypeStruct((bsz, seq, MLA_HEADS * MLA_V_DIM), BF16),
        scratch_shapes=[pltpu.VMEM((nh, tq, tq), F32), pltpu.VMEM((nh, FAST_CHUNKS * tq, tq), BF16),
                        pltpu.VMEM((nh, VT_ROWS, tq), F32), pltpu.VMEM((nh, VT_ROWS, tq), F32)],
        compiler_params=_params("parallel", "parallel", "arbitrary"),
        name="mla_attention",
    )(q, k, vt)


def _layer_tail_kernel(x_ref, ao_ref, so_ref, gate_ref, kx_ref, vx_ref, lng_ref, lnb_ref, w_oa_ref, w_o_ref,
                       ln1g_ref, ln1b_ref, w_xq_ref, w_xo_ref, ln2g_ref, ln2b_ref, w_up_ref, w_down_ref,
                       ln3g_ref, ln3b_ref, out_ref, h2_scr, *, x_scale, chunk):
    @pl.when(pl.program_id(0) == 0)
    def _():
        h2_scr[...] = jnp.zeros_like(h2_scr)

    h2_prev = h2_scr[...]
    hb_prev = h2_prev.astype(BF16)
    nchunk = w_up_ref.shape[-1] // chunk

    def mlp_units():
        ff = jnp.zeros_like(h2_prev)
        for c in range(nchunk):
            cols = slice(c * chunk, (c + 1) * chunk)
            a = jnp.maximum(_dot(hb_prev, w_up_ref[:, cols]), 0.0)
            a = (a * a).astype(BF16)
            yield None
            ff = ff + _dot(a, w_down_ref[cols, :])
            yield ff

    units = mlp_units()
    ff = [None]

    def issue(n):
        for _ in range(n):
            ff[0] = next(units, ff[0])

    h0 = _layer_norm(x_ref[...], lng_ref[...], lnb_ref[...])
    d = h0.shape[-1]
    a_out = _dot(ao_ref[...], w_oa_ref[...])
    issue(1)
    gate = gate_ref[...]
    mixed = gate[:, :d].astype(F32) * so_ref[...].astype(F32) + gate[:, d:].astype(F32) * a_out
    mix = _dot(mixed.astype(BF16), w_o_ref[...])
    issue(2)
    h1 = _layer_norm(DN_ALPHA * h0 + mix, ln1g_ref[...], ln1b_ref[...])
    qx = (_dot(h1.astype(BF16), w_xq_ref[...]) * x_scale).astype(BF16)
    issue(2)
    hd = d // XATTN_HEADS
    xa = jnp.zeros_like(h1)
    for hh in range(XATTN_HEADS):
        cols = slice(hh * hd, (hh + 1) * hd)
        s = _dot_nt(qx[:, cols], kx_ref[:, cols])
        issue(1)
        p = jnp.exp(s - jnp.max(s, axis=-1, keepdims=True))
        o = _dot(p.astype(BF16), vx_ref[:, cols]) / jnp.sum(p, axis=-1, keepdims=True)
        xa = xa + _dot(o.astype(BF16), w_xo_ref[cols, :])
    issue(2 * nchunk)
    h2 = _layer_norm(DN_ALPHA * h1 + xa, ln2g_ref[...], ln2b_ref[...])
    out_ref[...] = _layer_norm(DN_ALPHA * h2_prev + ff[0], ln3g_ref[...], ln3b_ref[...])
    h2_scr[...] = h2


def _layer_tail(x, ao, so, gate, kx, vx, ln_g, ln_b, w_oa, w_o, ln1_g, ln1_b, w_xq, w_xo, ln2_g, ln2_b,
                w_up, w_down, ln3_g, ln3_b, tm):
    bsz, seq, d = x.shape
    m = kx.shape[1]
    hidden = w_up.shape[-1]
    per_seq = seq // tm
    ntile = bsz * per_seq
    vec = lambda a: a.reshape(1, d)
    flat = lambda a: a.reshape(bsz * seq, a.shape[-1])
    cur = lambda i: jnp.minimum(i, ntile - 1)
    tok = lambda w: pl.BlockSpec((tm, w), lambda i: (cur(i), 0))
    per_b = pl.BlockSpec((None, m, d), lambda i: (cur(i) // per_seq, 0, 0))
    consts = [vec(ln_g), vec(ln_b), w_oa.astype(BF16), w_o.astype(BF16), vec(ln1_g), vec(ln1_b),
              w_xq.astype(BF16), w_xo.astype(BF16), vec(ln2_g), vec(ln2_b),
              w_up.astype(BF16), w_down.astype(BF16), vec(ln3_g), vec(ln3_b)]
    out = pl.pallas_call(
        functools.partial(_layer_tail_kernel, x_scale=(d // XATTN_HEADS) ** -0.5, chunk=min(hidden, MLP_CHUNK)),
        grid=(ntile + 1,),
        in_specs=[tok(d), tok(ao.shape[-1]), tok(d), tok(2 * d), per_b, per_b] + [_const_spec(c.shape) for c in consts],
        out_specs=pl.BlockSpec((tm, d), lambda i: (jnp.maximum(i - 1, 0), 0)),
        out_shape=jax.ShapeDtypeStruct((bsz * seq, d), F32),
        scratch_shapes=[pltpu.VMEM((tm, d), F32)],
        compiler_params=_params("arbitrary"),
        name="layer_tail",
    )(flat(x), flat(ao), flat(so), flat(gate), kx, vx, *consts)
    return out.reshape(bsz, seq, d)


def kernel(x, mem, positions, ln_in_g, ln_in_b, w_in, s5_lam_re, s5_lam_im, s5_log_dt, s5_b_re, s5_b_im,
           s5_c_re, s5_c_im, s5_d, w_glu, q_norm_g, w_uq, kv_norm_g, w_ukv, w_oa, w_o, ln1_g, ln1_b,
           w_xq, w_xk, w_xv, w_xo, ln2_g, ln2_b, w_up, w_down, ln3_g, ln3_b):
    bsz, seq, d = x.shape
    n = bsz * seq
    assert w_in.shape[0] == DEPTH == 1
    tm = min(seq, 512)
    tq = min(seq, 512)
    nh = 4
    tt = min(seq, 128)

    cos, sin = _rope_tables(positions)
    ones = jnp.ones((bsz, seq, MLA_NOPE_DIM), F32)
    zeros = jnp.zeros((bsz, seq, LANE - MLA_QK_DIM), F32)
    cos128 = jnp.concatenate([ones, cos, cos, zeros], axis=-1)
    sin128 = jnp.concatenate([0.0 * ones, sin, sin, zeros], axis=-1)

    assert tm == tq
    u, q, k, vt, gate = _in_proj(x, cos128, sin128, ln_in_g, ln_in_b, w_in[0],
                                 q_norm_g[0], w_uq[0], kv_norm_g[0], w_ukv[0], tm, nh)

    ab_re, ab_im, bb_re, bb_im = _s5_discretize(s5_lam_re[0], s5_lam_im[0], s5_log_dt[0], s5_b_re[0], s5_b_im[0])
    s_out = _s5_scan(u, ab_re, ab_im, bb_re, bb_im, s5_c_re[0], s5_c_im[0],
                     s5_d[0], w_glu[0], tt)

    a_o = _attention(q, k, vt, tq, nh)

    kx, vx = _mem_kv(mem, w_xk[0], w_xv[0])
    return _layer_tail(x, a_o, s_out, gate, kx, vx, ln_in_g, ln_in_b, w_oa[0], w_o[0], ln1_g[0], ln1_b[0],
                       w_xq[0], w_xo[0], ln2_g[0], ln2_b[0], w_up[0], w_down[0], ln3_g[0], ln3_b[0], tm)
```

```python
import functools
import math

import jax
import jax.numpy as jnp
from jax import lax
from jax.experimental import pallas as pl
from jax.experimental.pallas import tpu as pltpu

F32 = jnp.float32
BF16 = jnp.bfloat16

S5_GROUP_CH = 16
S5_STATE = 64
S5_MAX_RE = -1e-4
MLA_HEADS = 8
MLA_NOPE_DIM = 64
MLA_ROPE_DIM = 32
MLA_QK_DIM = MLA_NOPE_DIM + MLA_ROPE_DIM
MLA_V_DIM = 64
ROPE_THETA = 10000.0
XATTN_HEADS = 4
LN_EPS = 1e-5
RMS_EPS = 1e-6
NEG_INF = -1e30
DEPTH = 1
DN_ALPHA = (2.0 * DEPTH) ** 0.25

SCAN_UNROLL = 4
MAX_CHAINS = 4
MLP_CHUNK = 1024
QK_AHEAD = 2
FAST_CHUNKS = 2
REF_MARGIN = 64.0
F32_SUBLANES = 8
BF16_SUBLANES = 16
VT_ROWS = MLA_V_DIM + BF16_SUBLANES
LANE = 128
VMEM_LIMIT = 56 * 1024 * 1024


def _const_spec(shape):
    nd = len(shape)
    return pl.BlockSpec(shape, lambda *_: (0,) * nd, pipeline_mode=pl.Buffered(1))


def _params(*sem, flags=None):
    return pltpu.CompilerParams(dimension_semantics=sem, vmem_limit_bytes=VMEM_LIMIT, flags=flags)


def _layer_norm(x, g, b):
    mu = jnp.mean(x, axis=-1, keepdims=True)
    xc = x - mu
    var = jnp.mean(xc * xc, axis=-1, keepdims=True)
    return xc * lax.rsqrt(var + LN_EPS) * g + b


def _rms_norm(x, g):
    return x * lax.rsqrt(jnp.mean(x * x, axis=-1, keepdims=True) + RMS_EPS) * g


def _gelu_exact(x):
    return 0.5 * x * (1.0 + lax.erf(x * (0.5 ** 0.5)))


def _dot(a, b):
    return jnp.dot(a, b, preferred_element_type=F32)


def _dot_nt(a, b):
    return lax.dot_general(a, b, (((1,), (1,)), ((), ())), preferred_element_type=F32)


def _s5_disc_kernel(lr_ref, li_ref, ldt_ref, br_ref, bi_ref, are_ref, aim_ref, bbr_ref, bbi_ref):
    lr = jnp.minimum(lr_ref[...], S5_MAX_RE)
    li = li_ref[...]
    dt = jnp.exp(ldt_ref[...])
    mag = jnp.exp(lr * dt)
    ang = li * dt
    ab_re = mag * jnp.cos(ang)
    ab_im = mag * jnp.sin(ang)
    den = lr * lr + li * li
    nr = ab_re - 1.0
    f_re = (nr * lr + ab_im * li) / den
    f_im = (ab_im * lr - nr * li) / den
    br = br_ref[...]
    bi = bi_ref[...]
    are_ref[...] = ab_re
    aim_ref[...] = ab_im
    bbr_ref[...] = f_re * br - f_im * bi
    bbi_ref[...] = f_re * bi + f_im * br


def _s5_discretize(lam_re, lam_im, log_dt, b_re, b_im):
    g, p = lam_re.shape
    h = b_re.shape[-1]
    n = g * p
    col = lambda a: a.reshape(n, 1)
    ldt = jnp.broadcast_to(log_dt[:, None], (g, p))
    outs = pl.pallas_call(
        _s5_disc_kernel,
        out_shape=[jax.ShapeDtypeStruct((n, 1), F32), jax.ShapeDtypeStruct((n, 1), F32),
                   jax.ShapeDtypeStruct((n, h), F32), jax.ShapeDtypeStruct((n, h), F32)],
        name="s5_discretize",
    )(col(lam_re), col(lam_im), col(ldt), b_re.reshape(n, h), b_im.reshape(n, h))
    ab_re, ab_im, bb_re, bb_im = outs
    return ab_re.reshape(g, p), ab_im.reshape(g, p), bb_re.reshape(g, p, h), bb_im.reshape(g, p, h)


def _rope_kernel(pos_ref, inv_ref, cos_ref, sin_ref):
    ang = pos_ref[...].astype(F32) * inv_ref[...]
    cos_ref[...] = jnp.cos(ang)
    sin_ref[...] = jnp.sin(ang)


def _rope_tables(positions):
    bsz, seq = positions.shape
    half = MLA_ROPE_DIM // 2
    per_row = LANE // half
    rows = bsz * seq // per_row
    inv = ROPE_THETA ** (-jnp.arange(0, MLA_ROPE_DIM, 2, dtype=F32) / MLA_ROPE_DIM)
    inv_row = jnp.tile(inv, per_row).reshape(1, LANE)
    pos_rep = jnp.repeat(positions.reshape(-1), half).reshape(rows, LANE)
    tr = min(rows, 1024)
    cos, sin = pl.pallas_call(
        _rope_kernel,
        grid=(rows // tr,),
        in_specs=[pl.BlockSpec((tr, LANE), lambda i: (i, 0)), _const_spec((1, LANE))],
        out_specs=[pl.BlockSpec((tr, LANE), lambda i: (i, 0))] * 2,
        out_shape=[jax.ShapeDtypeStruct((rows, LANE), F32)] * 2,
        compiler_params=_params("parallel"),
        name="rope_table",
    )(pos_rep, inv_row)
    return cos.reshape(bsz, seq, half), sin.reshape(bsz, seq, half)


def _mem_kv_kernel(mem_ref, wk_ref, wv_ref, k_ref, v_ref):
    m = mem_ref[...].astype(BF16)
    k_ref[...] = _dot(m, wk_ref[...]).astype(BF16)
    v_ref[...] = _dot(m, wv_ref[...]).astype(BF16)


def _mem_kv(mem, w_xk, w_xv):
    bsz, m, d = mem.shape
    return pl.pallas_call(
        _mem_kv_kernel,
        grid=(bsz,),
        in_specs=[pl.BlockSpec((None, m, d), lambda b: (b, 0, 0)), _const_spec((d, d)), _const_spec((d, d))],
        out_specs=[pl.BlockSpec((None, m, d), lambda b: (b, 0, 0))] * 2,
        out_shape=[jax.ShapeDtypeStruct((bsz, m, d), BF16)] * 2,
        compiler_params=_params("parallel"),
        name="mem_kv",
    )(mem, w_xk.astype(BF16), w_xv.astype(BF16))


def _in_proj_kernel(x_ref, cos_ref, sin_ref, lng_ref, lnb_ref, w_u_ref, w_cq_ref, w_ckv_ref, w_kr_ref,
                    w_g_ref, qg_ref, kvg_ref, w_q_ref, w_qs_ref, w_k_ref, w_vt_ref,
                    u_ref, q_ref, k_ref, vt_ref, gate_ref, *, q_scale, nh):
    hb = _layer_norm(x_ref[...], lng_ref[...], lnb_ref[...]).astype(BF16)
    cos = cos_ref[...]
    sin = sin_ref[...]
    gate_cols = gate_ref.shape[-1] // 4

    def gate_quarter(c):
        cols = slice(c * gate_cols, (c + 1) * gate_cols)
        gate_ref[:, cols] = jax.nn.sigmoid(_dot(hb, w_g_ref[:, cols])).astype(BF16)

    cq = _dot(hb, w_cq_ref[...])
    ckv = _dot(hb, w_ckv_ref[...])
    gate_quarter(0)
    cqn = _rms_norm(cq, qg_ref[...]).astype(BF16)
    qa = _dot(cqn, w_q_ref[...])
    qb = _dot(cqn, w_qs_ref[...])
    gate_quarter(1)
    ckvn = _rms_norm(ckv, kvg_ref[...]).astype(BF16)
    kn = _dot(ckvn, w_k_ref[...])
    kr2 = _dot(hb, w_kr_ref[...])
    k_rope = kr2[:, :LANE] * cos + kr2[:, LANE:] * sin
    vt = _dot_nt(w_vt_ref[...], ckvn)
    gate_quarter(2)
    gate_quarter(3)
    u_ref[...] = _dot(hb, w_u_ref[...])
    ones = jnp.ones((VT_ROWS - MLA_V_DIM, vt.shape[-1]), BF16)
    for h in range(MLA_HEADS):
        g, r0 = h // nh, (h % nh) * VT_ROWS
        vt_ref[g, r0:r0 + MLA_V_DIM, :] = vt[h * MLA_V_DIM:(h + 1) * MLA_V_DIM, :].astype(BF16)
        vt_ref[g, r0 + MLA_V_DIM:r0 + VT_ROWS, :] = ones
    for h in range(MLA_HEADS):
        sl = slice(h * LANE, (h + 1) * LANE)
        q_ref[:, sl] = ((qa[:, sl] * cos + qb[:, sl] * sin) * q_scale).astype(BF16)
        k_ref[:, sl] = (kn[:, sl] + k_rope).astype(BF16)


def _rotate_half_cols(w):
    half = MLA_ROPE_DIM // 2
    return jnp.concatenate([-w[..., half:], w[..., :half]], axis=-1)


def _head_slab(nope, rope):
    d, h, _ = nope.shape
    pad = jnp.zeros((d, h, LANE - MLA_QK_DIM), nope.dtype)
    return jnp.concatenate([nope, rope, pad], axis=-1).reshape(d, h * LANE)


def _in_proj(x, cos128, sin128, ln_g, ln_b, w_in, q_norm_g, w_uq, kv_norm_g, w_ukv, tm, nh):
    bsz, seq, d = x.shape
    groups = MLA_HEADS // nh
    s5w = q_rank = kv_rank = d // 4
    o1, o2, o3 = s5w, s5w + q_rank, s5w + q_rank + kv_rank
    o4 = o3 + MLA_ROPE_DIM
    hl = MLA_HEADS * LANE
    w_u = w_in[:, :o1].astype(BF16)
    w_cq = w_in[:, o1:o2].astype(BF16)
    w_ckv = w_in[:, o2:o3].astype(BF16)
    w_kr = w_in[:, o3:o4]
    zl = jnp.zeros((d, MLA_NOPE_DIM), F32)
    zr = jnp.zeros((d, LANE - MLA_QK_DIM), F32)
    w_kr2 = jnp.concatenate([zl, w_kr, zr, zl, _rotate_half_cols(w_kr), zr], axis=-1).astype(BF16)
    w_g = w_in[:, o4:].astype(BF16)
    wq3 = w_uq.reshape(q_rank, MLA_HEADS, MLA_QK_DIM)
    wq_nope, wq_rope = wq3[..., :MLA_NOPE_DIM], wq3[..., MLA_NOPE_DIM:]
    w_q = _head_slab(wq_nope, wq_rope).astype(BF16)
    w_qs = _head_slab(jnp.zeros_like(wq_nope), _rotate_half_cols(wq_rope)).astype(BF16)
    wkv3 = w_ukv.reshape(kv_rank, MLA_HEADS, MLA_NOPE_DIM + MLA_V_DIM)
    w_k = _head_slab(wkv3[..., :MLA_NOPE_DIM],
                     jnp.zeros((kv_rank, MLA_HEADS, MLA_ROPE_DIM), F32)).astype(BF16)
    w_vt = wkv3[..., MLA_NOPE_DIM:].reshape(kv_rank, MLA_HEADS * MLA_V_DIM).T.astype(BF16)
    row = lambda w: pl.BlockSpec((None, tm, w), lambda b, i: (b, i, 0))
    tok = lambda w, dt: jax.ShapeDtypeStruct((bsz, seq, w), dt)
    consts = [ln_g.reshape(1, d), ln_b.reshape(1, d), w_u, w_cq, w_ckv, w_kr2, w_g,
              q_norm_g.reshape(1, q_rank), kv_norm_g.reshape(1, kv_rank), w_q, w_qs, w_k, w_vt]
    return pl.pallas_call(
        functools.partial(_in_proj_kernel, q_scale=MLA_QK_DIM ** -0.5 * math.log2(math.e), nh=nh),
        grid=(bsz, seq // tm),
        in_specs=[row(d), row(LANE), row(LANE)] + [_const_spec(c.shape) for c in consts],
        out_specs=[row(s5w), row(hl), row(hl),
                   pl.BlockSpec((None, None, groups, nh * VT_ROWS, tm), lambda b, i: (b, i, 0, 0, 0)), row(2 * d)],
        out_shape=[tok(s5w, F32), tok(hl, BF16), tok(hl, BF16),
                   jax.ShapeDtypeStruct((bsz, seq // tm, groups, nh * VT_ROWS, tm), BF16), tok(2 * d, BF16)],
        compiler_params=_params("parallel", "parallel"),
        name="in_proj",
    )(x, cos128, sin128, *consts)


def _s5_scan_kernel(u_ref, bblk_ref, are_ref, aim_ref, cblk_ref, d_ref, wglu_ref, out_ref,
                    hbuf, st_re, st_im, *, bsz, tt, nstate):
    @pl.when(pl.program_id(0) == 0)
    def _():
        st_re[...] = jnp.zeros_like(st_re)
        st_im[...] = jnp.zeros_like(st_im)

    width = u_ref.shape[-1]
    u = jnp.swapaxes(u_ref[...], 0, 1).reshape(tt * bsz, width)
    bu = _dot(u.astype(BF16), bblk_ref[...])
    nt = nstate // LANE
    for c in range(2 * nt):
        hbuf[c] = bu[:, c * LANE:(c + 1) * LANE]
    ar = [jnp.broadcast_to(are_ref[:, c * LANE:(c + 1) * LANE], (bsz, LANE)) for c in range(nt)]
    ai = [jnp.broadcast_to(aim_ref[:, c * LANE:(c + 1) * LANE], (bsz, LANE)) for c in range(nt)]

    def step(t, carry):
        hr, hi = carry
        rows = pl.ds(pl.multiple_of(t * bsz, bsz), bsz)
        nr, ni = [], []
        for c in range(nt):
            r = ar[c] * hr[c] - ai[c] * hi[c] + hbuf[c, rows, :]
            i = ar[c] * hi[c] + ai[c] * hr[c] + hbuf[nt + c, rows, :]
            hbuf[c, rows, :] = r
            hbuf[nt + c, rows, :] = i
            nr.append(r)
            ni.append(i)
        return tuple(nr), tuple(ni)

    def steps(tb, carry):
        for k in range(SCAN_UNROLL):
            carry = step(tb * SCAN_UNROLL + k, carry)
        return carry

    init = (tuple(st_re[c] for c in range(nt)), tuple(st_im[c] for c in range(nt)))
    hr, hi = lax.fori_loop(0, tt // SCAN_UNROLL, steps, init)
    for c in range(nt):
        st_re[c] = hr[c]
        st_im[c] = hi[c]
    half = tt * bsz // 2
    ys = []
    for r in range(2):
        rows = slice(r * half, (r + 1) * half)
        hs = jnp.concatenate([hbuf[c, rows, :].astype(BF16) for c in range(2 * nt)], axis=-1)
        ys.append(_dot(hs, cblk_ref[...]))
    y = jnp.concatenate(ys, axis=0) + d_ref[...] * u
    y = jnp.swapaxes(y.reshape(tt, bsz, width), 0, 1).reshape(bsz * tt, width)
    z = _dot(_gelu_exact(y).astype(BF16), wglu_ref[...])
    dm = z.shape[-1] // 2
    s_out = z[:, :dm] * jax.nn.sigmoid(z[:, dm:])
    out_ref[...] = s_out.reshape(bsz, tt, dm).astype(BF16)


def _s5_scan(u3, ab_re, ab_im, bb_re, bb_im, c_re, c_im, d_skip, w_glu, tt):
    bsz, seq, width = u3.shape
    g, p, h = bb_re.shape
    nstate = g * p
    eye = jnp.eye(g, dtype=F32)
    blk_b = lambda bb: jnp.einsum('gph,gk->ghkp', bb, eye).reshape(g * h, nstate)
    bblk = jnp.concatenate([blk_b(bb_re), blk_b(bb_im)], axis=1).astype(BF16)
    blk_c = lambda c: jnp.einsum('ghp,gk->gpkh', c, eye).reshape(nstate, g * h)
    cblk = jnp.concatenate([blk_c(c_re), -blk_c(c_im)], axis=0).astype(BF16)
    dm2 = w_glu.shape[-1]
    consts = [bblk, ab_re.reshape(1, nstate), ab_im.reshape(1, nstate), cblk,
              d_skip.reshape(1, width), w_glu.astype(BF16)]
    return pl.pallas_call(
        functools.partial(_s5_scan_kernel, bsz=bsz, tt=tt, nstate=nstate),
        grid=(seq // tt,),
        in_specs=[pl.BlockSpec((bsz, tt, width), lambda i: (0, i, 0))] + [_const_spec(c.shape) for c in consts],
        out_specs=pl.BlockSpec((bsz, tt, dm2 // 2), lambda i: (0, i, 0)),
        out_shape=jax.ShapeDtypeStruct((bsz, seq, dm2 // 2), BF16),
        scratch_shapes=[pltpu.VMEM((2 * nstate // LANE, bsz * tt, LANE), F32),
                        pltpu.VMEM((nstate // LANE, bsz, LANE), F32),
                        pltpu.VMEM((nstate // LANE, bsz, LANE), F32)],
        compiler_params=_params("arbitrary"),
        name="s5_scan",
    )(u3, *consts)


def _attn_kernel(q_ref, k_ref, vt_ref, o_ref, s_scr, p_scr, acc_scr, acc_new_scr, *, tq, nh):
    qi = pl.program_id(2)
    heads = [slice(h * LANE, (h + 1) * LANE) for h in range(nh)]
    acc_scr[...] = jnp.zeros_like(acc_scr)

    def scores(blk, h, masked=False, nblk=1):
        rows = pl.ds(pl.multiple_of(blk * tq, tq), nblk * tq)
        s = _dot_nt(k_ref[rows, heads[h]], q_ref[:, heads[h]])
        if masked:
            keep = (lax.broadcasted_iota(jnp.int32, (tq, tq), 0) <= lax.broadcasted_iota(jnp.int32, (tq, tq), 1))
            s = jnp.where(keep, s, NEG_INF)
        return s

    def values(blk, h):
        return vt_ref[blk, h * VT_ROWS:(h + 1) * VT_ROWS, :]

    def two_pass_chunk(blk, ms, masked):
        for h in range(min(QK_AHEAD, nh)):
            s_scr[h] = scores(blk, h, masked)
        new_ms = []
        for h in range(nh):
            if h + QK_AHEAD < nh:
                s_scr[h + QK_AHEAD] = scores(blk, h + QK_AHEAD, masked)
            tiles = [s_scr[h, r * F32_SUBLANES:(r + 1) * F32_SUBLANES, :] for r in range(tq // F32_SUBLANES)]
            part = tiles[:MAX_CHAINS]
            for r, t in enumerate(tiles[MAX_CHAINS:]):
                part[r % MAX_CHAINS] = jnp.maximum(part[r % MAX_CHAINS], t)
            m8 = functools.reduce(jnp.maximum, part)
            m_new = jnp.maximum(ms[h], jnp.max(m8, axis=0, keepdims=True))
            for r in range(tq // BF16_SUBLANES):
                sl = slice(r * BF16_SUBLANES, (r + 1) * BF16_SUBLANES)
                p_scr[h, sl, :] = jnp.exp2(s_scr[h, sl, :] - m_new).astype(BF16)
            acc_scr[h] = jnp.exp2(ms[h] - m_new) * acc_scr[h] + _dot(values(blk, h), p_scr[h, :tq, :])
            new_ms.append(m_new)
        return tuple(new_ms)

    def single_pass(blk, nblk, ms):
        excess = []

        def softmax(h):
            s = scores(blk, h, nblk=nblk)
            p_scr[h, :nblk * tq, :] = jnp.exp2(s - ms[h]).astype(BF16)
            excess.append(jnp.max(s, axis=0, keepdims=True) - ms[h])

        def accumulate(h):
            acc = acc_scr[h]
            for i in range(nblk):
                acc = acc + _dot(values(blk + i, h), p_scr[h, i * tq:(i + 1) * tq, :])
            acc_new_scr[h] = acc

        softmax(0)
        for h in range(1, nh):
            softmax(h)
            accumulate(h - 1)
        accumulate(nh - 1)
        return jnp.max(functools.reduce(jnp.maximum, excess))

    def step(blk, nblk, ms):
        worst = single_pass(blk, nblk, ms)

        def redo(ms):
            for i in range(nblk):
                ms = two_pass_chunk(blk + i, ms, False)
            return ms

        def commit(ms):
            acc_scr[...] = acc_new_scr[...]
            return ms

        return lax.cond(worst > REF_MARGIN, redo, commit, ms)

    ms = (jnp.full((1, tq), NEG_INF, F32),) * nh
    ms = two_pass_chunk(qi, ms, True)
    nwide = qi // FAST_CHUNKS
    ms = lax.fori_loop(0, nwide, lambda t, c: step(t * FAST_CHUNKS, FAST_CHUNKS, c), ms)
    lax.fori_loop(nwide * FAST_CHUNKS, qi, lambda j, c: step(j, 1, c), ms)
    ot = jnp.concatenate([acc_scr[h, :MLA_V_DIM, :] / acc_scr[h, MLA_V_DIM:MLA_V_DIM + 1, :] for h in range(nh)],
                         axis=0)
    o_ref[...] = ot.T.astype(BF16)


def _attention(q, k, vt, tq, nh):
    bsz, seq, _ = q.shape
    groups = MLA_HEADS // nh
    return pl.pallas_call(
        functools.partial(_attn_kernel, tq=tq, nh=nh),
        grid=(bsz, groups, seq // tq),
        in_specs=[pl.BlockSpec((None, tq, nh * LANE), lambda b, h, i: (b, i, h)),
                  pl.BlockSpec((None, seq, nh * LANE), lambda b, h, i: (b, 0, h)),
                  pl.BlockSpec((None, seq // tq, None, nh * VT_ROWS, tq), lambda b, h, i: (b, 0, h, 0, 0))],
        out_specs=pl.BlockSpec((None, tq, nh * MLA_V_DIM), lambda b, h, i: (b, i, h)),
        out_shape=jax.ShapeDtypeStruct((bsz, seq, MLA_HEADS * MLA_V_DIM), BF16),
        scratch_shapes=[pltpu.VMEM((nh, tq, tq), F32), pltpu.VMEM((nh, FAST_CHUNKS * tq, tq), BF16),
                        pltpu.VMEM((nh, VT_ROWS, tq), F32), pltpu.VMEM((nh, VT_ROWS, tq), F32)],
        compiler_params=_params("parallel", "parallel", "arbitrary"),
        name="mla_attention",
    )(q, k, vt)


def _layer_tail_kernel(x_ref, ao_ref, so_ref, gate_ref, kx_ref, vx_ref, lng_ref, lnb_ref, w_oa_ref, w_o_ref,
                       ln1g_ref, ln1b_ref, w_xq_ref, w_xo_ref, ln2g_ref, ln2b_ref, w_up_ref, w_down_ref,
                       ln3g_ref, ln3b_ref, out_ref, h2_scr, *, x_scale, chunk):
    @pl.when(pl.program_id(0) == 0)
    def _():
        h2_scr[...] = jnp.zeros_like(h2_scr)

    h2_prev = h2_scr[...]
    hb_prev = h2_prev.astype(BF16)
    nchunk = w_up_ref.shape[-1] // chunk

    def mlp_units():
        ff = jnp.zeros_like(h2_prev)
        for c in range(nchunk):
            cols = slice(c * chunk, (c + 1) * chunk)
            a = jnp.maximum(_dot(hb_prev, w_up_ref[:, cols]), 0.0)
            a = (a * a).astype(BF16)
            yield None
            ff = ff + _dot(a, w_down_ref[cols, :])
            yield ff

    units = mlp_units()
    ff = [None]

    def issue(n):
        for _ in range(n):
            ff[0] = next(units, ff[0])

    h0 = _layer_norm(x_ref[...], lng_ref[...], lnb_ref[...])
    d = h0.shape[-1]
    a_out = _dot(ao_ref[...], w_oa_ref[...])
    issue(1)
    gate = gate_ref[...]
    mixed = gate[:, :d].astype(F32) * so_ref[...].astype(F32) + gate[:, d:].astype(F32) * a_out
    mix = _dot(mixed.astype(BF16), w_o_ref[...])
    issue(2)
    h1 = _layer_norm(DN_ALPHA * h0 + mix, ln1g_ref[...], ln1b_ref[...])
    qx = (_dot(h1.astype(BF16), w_xq_ref[...]) * x_scale).astype(BF16)
    issue(2)
    hd = d // XATTN_HEADS
    xa = jnp.zeros_like(h1)
    for hh in range(XATTN_HEADS):
        cols = slice(hh * hd, (hh + 1) * hd)
        s = _dot_nt(qx[:, cols], kx_ref[:, cols])
        issue(1)
        p = jnp.exp(s - jnp.max(s, axis=-1, keepdims=True))
        o = _dot(p.astype(BF16), vx_ref[:, cols]) / jnp.sum(p, axis=-1, keepdims=True)
        xa = xa + _dot(o.astype(BF16), w_xo_ref[cols, :])
    issue(2 * nchunk)
    h2 = _layer_norm(DN_ALPHA * h1 + xa, ln2g_ref[...], ln2b_ref[...])
    out_ref[...] = _layer_norm(DN_ALPHA * h2_prev + ff[0], ln3g_ref[...], ln3b_ref[...])
    h2_scr[...] = h2


def _layer_tail(x, ao, so, gate, kx, vx, ln_g, ln_b, w_oa, w_o, ln1_g, ln1_b, w_xq, w_xo, ln2_g, ln2_b,
                w_up, w_down, ln3_g, ln3_b, tm):
    bsz, seq, d = x.shape
    m = kx.shape[1]
    hidden = w_up.shape[-1]
    per_seq = seq // tm
    ntile = bsz * per_seq
    vec = lambda a: a.reshape(1, d)
    flat = lambda a: a.reshape(bsz * seq, a.shape[-1])
    cur = lambda i: jnp.minimum(i, ntile - 1)
    tok = lambda w: pl.BlockSpec((tm, w), lambda i: (cur(i), 0))
    per_b = pl.BlockSpec((None, m, d), lambda i: (cur(i) // per_seq, 0, 0))
    consts = [vec(ln_g), vec(ln_b), w_oa.astype(BF16), w_o.astype(BF16), vec(ln1_g), vec(ln1_b),
              w_xq.astype(BF16), w_xo.astype(BF16), vec(ln2_g), vec(ln2_b),
              w_up.astype(BF16), w_down.astype(BF16), vec(ln3_g), vec(ln3_b)]
    out = pl.pallas_call(
        functools.partial(_layer_tail_kernel, x_scale=(d // XATTN_HEADS) ** -0.5, chunk=min(hidden, MLP_CHUNK)),
        grid=(ntile + 1,),
        in_specs=[tok(d), tok(ao.shape[-1]), tok(d), tok(2 * d), per_b, per_b] + [_const_spec(c.shape) for c in consts],
        out_specs=pl.BlockSpec((tm, d), lambda i: (jnp.maximum(i - 1, 0), 0)),
        out_shape=jax.ShapeDtypeStruct((bsz * seq, d), F32),
        scratch_shapes=[pltpu.VMEM((tm, d), F32)],
        compiler_params=_params("arbitrary"),
        name="layer_tail",
    )(flat(x), flat(ao), flat(so), flat(gate), kx, vx, *consts)
    return out.reshape(bsz, seq, d)


def kernel(x, mem, positions, ln_in_g, ln_in_b, w_in, s5_lam_re, s5_lam_im, s5_log_dt, s5_b_re, s5_b_im,
           s5_c_re, s5_c_im, s5_d, w_glu, q_norm_g, w_uq, kv_norm_g, w_ukv, w_oa, w_o, ln1_g, ln1_b,
           w_xq, w_xk, w_xv, w_xo, ln2_g, ln2_b, w_up, w_down, ln3_g, ln3_b):
    bsz, seq, d = x.shape
    n = bsz * seq
    assert w_in.shape[0] == DEPTH == 1
    tm = min(seq, 512)
    tq = min(seq, 512)
    nh = 4
    tt = min(seq, 128)

    cos, sin = _rope_tables(positions)
    ones = jnp.ones((bsz, seq, MLA_NOPE_DIM), F32)
    zeros = jnp.zeros((bsz, seq, LANE - MLA_QK_DIM), F32)
    cos128 = jnp.concatenate([ones, cos, cos, zeros], axis=-1)
    sin128 = jnp.concatenate([0.0 * ones, sin, sin, zeros], axis=-1)

    assert tm == tq
    u, q, k, vt, gate = _in_proj(x, cos128, sin128, ln_in_g, ln_in_b, w_in[0],
                                 q_norm_g[0], w_uq[0], kv_norm_g[0], w_ukv[0], tm, nh)

    ab_re, ab_im, bb_re, bb_im = _s5_discretize(s5_lam_re[0], s5_lam_im[0], s5_log_dt[0], s5_b_re[0], s5_b_im[0])
    s_out = _s5_scan(u, ab_re, ab_im, bb_re, bb_im, s5_c_re[0], s5_c_im[0],
                     s5_d[0], w_glu[0], tt)

    a_o = _attention(q, k, vt, tq, nh)

    kx, vx = _mem_kv(mem, w_xk[0], w_xv[0])
    return _layer_tail(x, a_o, s_out, gate, kx, vx, ln_in_g, ln_in_b, w_oa[0], w_o[0], ln1_g[0], ln1_b[0],
                       w_xq[0], w_xo[0], ln2_g[0], ln2_b[0], w_up[0], w_down[0], ln3_g[0], ln3_b[0], tm)
```

```python
import functools
import math

import jax
import jax.numpy as jnp
from jax import lax
from jax.experimental import pallas as pl
from jax.experimental.pallas import tpu as pltpu

F32 = jnp.float32
BF16 = jnp.bfloat16

S5_GROUP_CH = 16
S5_STATE = 64
S5_MAX_RE = -1e-4
MLA_HEADS = 8
MLA_NOPE_DIM = 64
MLA_ROPE_DIM = 32
MLA_QK_DIM = MLA_NOPE_DIM + MLA_ROPE_DIM
MLA_V_DIM = 64
ROPE_THETA = 10000.0
XATTN_HEADS = 4
LN_EPS = 1e-5
RMS_EPS = 1e-6
NEG_INF = -1e30
DEPTH = 1
DN_ALPHA = (2.0 * DEPTH) ** 0.25

MAX_CHAINS = 4
MLP_CHUNK = 1024
QK_AHEAD = 2
FAST_CHUNKS = 2
REF_MARGIN = 64.0
F32_SUBLANES = 8
BF16_SUBLANES = 16
VT_ROWS = MLA_V_DIM + BF16_SUBLANES
LANE = 128
VMEM_LIMIT = 56 * 1024 * 1024


def _const_spec(shape):
    nd = len(shape)
    return pl.BlockSpec(shape, lambda *_: (0,) * nd, pipeline_mode=pl.Buffered(1))


def _params(*sem, flags=None):
    return pltpu.CompilerParams(dimension_semantics=sem, vmem_limit_bytes=VMEM_LIMIT, flags=flags)


def _layer_norm(x, g, b):
    mu = jnp.mean(x, axis=-1, keepdims=True)
    xc = x - mu
    var = jnp.mean(xc * xc, axis=-1, keepdims=True)
    return xc * lax.rsqrt(var + LN_EPS) * g + b


def _rms_norm(x, g):
    return x * lax.rsqrt(jnp.mean(x * x, axis=-1, keepdims=True) + RMS_EPS) * g


def _gelu_exact(x):
    return 0.5 * x * (1.0 + lax.erf(x * (0.5 ** 0.5)))


def _dot(a, b):
    return jnp.dot(a, b, preferred_element_type=F32)


def _dot_nt(a, b):
    return lax.dot_general(a, b, (((1,), (1,)), ((), ())), preferred_element_type=F32)


def _s5_disc_kernel(lr_ref, li_ref, ldt_ref, br_ref, bi_ref, are_ref, aim_ref, bbr_ref, bbi_ref):
    lr = jnp.minimum(lr_ref[...], S5_MAX_RE)
    li = li_ref[...]
    dt = jnp.exp(ldt_ref[...])
    mag = jnp.exp(lr * dt)
    ang = li * dt
    ab_re = mag * jnp.cos(ang)
    ab_im = mag * jnp.sin(ang)
    den = lr * lr + li * li
    nr = ab_re - 1.0
    f_re = (nr * lr + ab_im * li) / den
    f_im = (ab_im * lr - nr * li) / den
    br = br_ref[...]
    bi = bi_ref[...]
    are_ref[...] = ab_re
    aim_ref[...] = ab_im
    bbr_ref[...] = f_re * br - f_im * bi
    bbi_ref[...] = f_re * bi + f_im * br


def _s5_discretize(lam_re, lam_im, log_dt, b_re, b_im):
    g, p = lam_re.shape
    h = b_re.shape[-1]
    n = g * p
    col = lambda a: a.reshape(n, 1)
    ldt = jnp.broadcast_to(log_dt[:, None], (g, p))
    outs = pl.pallas_call(
        _s5_disc_kernel,
        out_shape=[jax.ShapeDtypeStruct((n, 1), F32), jax.ShapeDtypeStruct((n, 1), F32),
                   jax.ShapeDtypeStruct((n, h), F32), jax.ShapeDtypeStruct((n, h), F32)],
        name="s5_discretize",
    )(col(lam_re), col(lam_im), col(ldt), b_re.reshape(n, h), b_im.reshape(n, h))
    ab_re, ab_im, bb_re, bb_im = outs
    return ab_re.reshape(g, p), ab_im.reshape(g, p), bb_re.reshape(g, p, h), bb_im.reshape(g, p, h)


def _rope_kernel(pos_ref, inv_ref, cos_ref, sin_ref):
    ang = pos_ref[...].astype(F32) * inv_ref[...]
    cos_ref[...] = jnp.cos(ang)
    sin_ref[...] = jnp.sin(ang)


def _rope_tables(positions):
    bsz, seq = positions.shape
    half = MLA_ROPE_DIM // 2
    per_row = LANE // half
    rows = bsz * seq // per_row
    inv = ROPE_THETA ** (-jnp.arange(0, MLA_ROPE_DIM, 2, dtype=F32) / MLA_ROPE_DIM)
    inv_row = jnp.tile(inv, per_row).reshape(1, LANE)
    pos_rep = jnp.repeat(positions.reshape(-1), half).reshape(rows, LANE)
    tr = min(rows, 1024)
    cos, sin = pl.pallas_call(
        _rope_kernel,
        grid=(rows // tr,),
        in_specs=[pl.BlockSpec((tr, LANE), lambda i: (i, 0)), _const_spec((1, LANE))],
        out_specs=[pl.BlockSpec((tr, LANE), lambda i: (i, 0))] * 2,
        out_shape=[jax.ShapeDtypeStruct((rows, LANE), F32)] * 2,
        compiler_params=_params("parallel"),
        name="rope_table",
    )(pos_rep, inv_row)
    return cos.reshape(bsz, seq, half), sin.reshape(bsz, seq, half)


def _mem_kv_kernel(mem_ref, wk_ref, wv_ref, k_ref, v_ref):
    m = mem_ref[...].astype(BF16)
    k_ref[...] = _dot(m, wk_ref[...]).astype(BF16)
    v_ref[...] = _dot(m, wv_ref[...]).astype(BF16)


def _mem_kv(mem, w_xk, w_xv):
    bsz, m, d = mem.shape
    return pl.pallas_call(
        _mem_kv_kernel,
        grid=(bsz,),
        in_specs=[pl.BlockSpec((None, m, d), lambda b: (b, 0, 0)), _const_spec((d, d)), _const_spec((d, d))],
        out_specs=[pl.BlockSpec((None, m, d), lambda b: (b, 0, 0))] * 2,
        out_shape=[jax.ShapeDtypeStruct((bsz, m, d), BF16)] * 2,
        compiler_params=_params("parallel"),
        name="mem_kv",
    )(mem, w_xk.astype(BF16), w_xv.astype(BF16))


def _in_proj_kernel(x_ref, cos_ref, sin_ref, lng_ref, lnb_ref, w_u_ref, w_cq_ref, w_ckv_ref, w_kr_ref,
                    w_g_ref, qg_ref, kvg_ref, w_q_ref, w_qs_ref, w_k_ref, w_vt_ref,
                    u_ref, q_ref, k_ref, vt_ref, gate_ref, *, q_scale, nh):
    hb = _layer_norm(x_ref[...], lng_ref[...], lnb_ref[...]).astype(BF16)
    cos = cos_ref[...]
    sin = sin_ref[...]
    gate_cols = gate_ref.shape[-1] // 4

    def gate_quarter(c):
        cols = slice(c * gate_cols, (c + 1) * gate_cols)
        gate_ref[:, cols] = jax.nn.sigmoid(_dot(hb, w_g_ref[:, cols])).astype(BF16)

    cq = _dot(hb, w_cq_ref[...])
    ckv = _dot(hb, w_ckv_ref[...])
    gate_quarter(0)
    cqn = _rms_norm(cq, qg_ref[...]).astype(BF16)
    qa = _dot(cqn, w_q_ref[...])
    qb = _dot(cqn, w_qs_ref[...])
    gate_quarter(1)
    ckvn = _rms_norm(ckv, kvg_ref[...]).astype(BF16)
    kn = _dot(ckvn, w_k_ref[...])
    kr2 = _dot(hb, w_kr_ref[...])
    k_rope = kr2[:, :LANE] * cos + kr2[:, LANE:] * sin
    vt = _dot_nt(w_vt_ref[...], ckvn)
    gate_quarter(2)
    gate_quarter(3)
    u_ref[...] = _dot(hb, w_u_ref[...])
    ones = jnp.ones((VT_ROWS - MLA_V_DIM, vt.shape[-1]), BF16)
    for h in range(MLA_HEADS):
        g, r0 = h // nh, (h % nh) * VT_ROWS
        vt_ref[g, r0:r0 + MLA_V_DIM, :] = vt[h * MLA_V_DIM:(h + 1) * MLA_V_DIM, :].astype(BF16)
        vt_ref[g, r0 + MLA_V_DIM:r0 + VT_ROWS, :] = ones
    for h in range(MLA_HEADS):
        sl = slice(h * LANE, (h + 1) * LANE)
        q_ref[:, sl] = ((qa[:, sl] * cos + qb[:, sl] * sin) * q_scale).astype(BF16)
        k_ref[:, sl] = (kn[:, sl] + k_rope).astype(BF16)


def _rotate_half_cols(w):
    half = MLA_ROPE_DIM // 2
    return jnp.concatenate([-w[..., half:], w[..., :half]], axis=-1)


def _head_slab(nope, rope):
    d, h, _ = nope.shape
    pad = jnp.zeros((d, h, LANE - MLA_QK_DIM), nope.dtype)
    return jnp.concatenate([nope, rope, pad], axis=-1).reshape(d, h * LANE)


def _in_proj(x, cos128, sin128, ln_g, ln_b, w_in, q_norm_g, w_uq, kv_norm_g, w_ukv, tm, nh):
    bsz, seq, d = x.shape
    groups = MLA_HEADS // nh
    s5w = q_rank = kv_rank = d // 4
    o1, o2, o3 = s5w, s5w + q_rank, s5w + q_rank + kv_rank
    o4 = o3 + MLA_ROPE_DIM
    hl = MLA_HEADS * LANE
    w_u = w_in[:, :o1].astype(BF16)
    w_cq = w_in[:, o1:o2].astype(BF16)
    w_ckv = w_in[:, o2:o3].astype(BF16)
    w_kr = w_in[:, o3:o4]
    zl = jnp.zeros((d, MLA_NOPE_DIM), F32)
    zr = jnp.zeros((d, LANE - MLA_QK_DIM), F32)
    w_kr2 = jnp.concatenate([zl, w_kr, zr, zl, _rotate_half_cols(w_kr), zr], axis=-1).astype(BF16)
    w_g = w_in[:, o4:].astype(BF16)
    wq3 = w_uq.reshape(q_rank, MLA_HEADS, MLA_QK_DIM)
    wq_nope, wq_rope = wq3[..., :MLA_NOPE_DIM], wq3[..., MLA_NOPE_DIM:]
    w_q = _head_slab(wq_nope, wq_rope).astype(BF16)
    w_qs = _head_slab(jnp.zeros_like(wq_nope), _rotate_half_cols(wq_rope)).astype(BF16)
    wkv3 = w_ukv.reshape(kv_rank, MLA_HEADS, MLA_NOPE_DIM + MLA_V_DIM)
    w_k = _head_slab(wkv3[..., :MLA_NOPE_DIM],
                     jnp.zeros((kv_rank, MLA_HEADS, MLA_ROPE_DIM), F32)).astype(BF16)
    w_vt = wkv3[..., MLA_NOPE_DIM:].reshape(kv_rank, MLA_HEADS * MLA_V_DIM).T.astype(BF16)
    row = lambda w: pl.BlockSpec((None, tm, w), lambda b, i: (b, i, 0))
    tok = lambda w, dt: jax.ShapeDtypeStruct((bsz, seq, w), dt)
    consts = [ln_g.reshape(1, d), ln_b.reshape(1, d), w_u, w_cq, w_ckv, w_kr2, w_g,
              q_norm_g.reshape(1, q_rank), kv_norm_g.reshape(1, kv_rank), w_q, w_qs, w_k, w_vt]
    return pl.pallas_call(
        functools.partial(_in_proj_kernel, q_scale=MLA_QK_DIM ** -0.5 * math.log2(math.e), nh=nh),
        grid=(bsz, seq // tm),
        in_specs=[row(d), row(LANE), row(LANE)] + [_const_spec(c.shape) for c in consts],
        out_specs=[row(s5w), row(hl), row(hl),
                   pl.BlockSpec((None, None, groups, nh * VT_ROWS, tm), lambda b, i: (b, i, 0, 0, 0)), row(2 * d)],
        out_shape=[tok(s5w, F32), tok(hl, BF16), tok(hl, BF16),
                   jax.ShapeDtypeStruct((bsz, seq // tm, groups, nh * VT_ROWS, tm), BF16), tok(2 * d, BF16)],
        compiler_params=_params("parallel", "parallel"),
        name="in_proj",
    )(x, cos128, sin128, *consts)


def _s5_scan_kernel(u_ref, bblk_ref, are_ref, aim_ref, cblk_ref, d_ref, wglu_ref, out_ref,
                    hbuf, st_re, st_im, *, bsz, tt, nsub, nstate):
    @pl.when(pl.program_id(0) == 0)
    def _():
        st_re[...] = jnp.zeros_like(st_re)
        st_im[...] = jnp.zeros_like(st_im)

    width = u_ref.shape[-1]
    nt = nstate // LANE
    ar = [jnp.broadcast_to(are_ref[:, c * LANE:(c + 1) * LANE], (bsz, LANE)) for c in range(nt)]
    ai = [jnp.broadcast_to(aim_ref[:, c * LANE:(c + 1) * LANE], (bsz, LANE)) for c in range(nt)]

    def project_in(sc):
        u = jnp.swapaxes(u_ref[:, sc * tt:(sc + 1) * tt, :], 0, 1).reshape(tt * bsz, width)
        bu = _dot(u.astype(BF16), bblk_ref[...])
        for c in range(2 * nt):
            hbuf[sc, c] = bu[:, c * LANE:(c + 1) * LANE]
        return u

    def recurrence(sc, hr, hi):
        for t in range(tt):
            rows = slice(t * bsz, (t + 1) * bsz)
            nr, ni = [], []
            for c in range(nt):
                r = ar[c] * hr[c] - ai[c] * hi[c] + hbuf[sc, c, rows, :]
                i = ar[c] * hi[c] + ai[c] * hr[c] + hbuf[sc, nt + c, rows, :]
                hbuf[sc, c, rows, :] = r
                hbuf[sc, nt + c, rows, :] = i
                nr.append(r)
                ni.append(i)
            hr, hi = nr, ni
        return hr, hi

    def project_out(sc, u):
        half = tt * bsz // 2
        ys = []
        for r in range(2):
            rows = slice(r * half, (r + 1) * half)
            hs = jnp.concatenate([hbuf[sc, c, rows, :].astype(BF16) for c in range(2 * nt)], axis=-1)
            ys.append(_dot(hs, cblk_ref[...]))
        y = jnp.concatenate(ys, axis=0) + d_ref[...] * u
        y = jnp.swapaxes(y.reshape(tt, bsz, width), 0, 1).reshape(bsz * tt, width)
        z = _dot(_gelu_exact(y).astype(BF16), wglu_ref[...])
        dm = z.shape[-1] // 2
        s_out = z[:, :dm] * jax.nn.sigmoid(z[:, dm:])
        out_ref[:, sc * tt:(sc + 1) * tt, :] = s_out.reshape(bsz, tt, dm).astype(BF16)

    us = [project_in(sc) for sc in range(nsub)]
    hr = [st_re[c] for c in range(nt)]
    hi = [st_im[c] for c in range(nt)]
    for sc in range(nsub):
        hr, hi = recurrence(sc, hr, hi)
        project_out(sc, us[sc])
    for c in range(nt):
        st_re[c] = hr[c]
        st_im[c] = hi[c]


def _s5_scan(u3, ab_re, ab_im, bb_re, bb_im, c_re, c_im, d_skip, w_glu, tt, nsub):
    bsz, seq, width = u3.shape
    g, p, h = bb_re.shape
    nstate = g * p
    eye = jnp.eye(g, dtype=F32)
    blk_b = lambda bb: jnp.einsum('gph,gk->ghkp', bb, eye).reshape(g * h, nstate)
    bblk = jnp.concatenate([blk_b(bb_re), blk_b(bb_im)], axis=1).astype(BF16)
    blk_c = lambda c: jnp.einsum('ghp,gk->gpkh', c, eye).reshape(nstate, g * h)
    cblk = jnp.concatenate([blk_c(c_re), -blk_c(c_im)], axis=0).astype(BF16)
    dm2 = w_glu.shape[-1]
    consts = [bblk, ab_re.reshape(1, nstate), ab_im.reshape(1, nstate), cblk,
              d_skip.reshape(1, width), w_glu.astype(BF16)]
    return pl.pallas_call(
        functools.partial(_s5_scan_kernel, bsz=bsz, tt=tt, nsub=nsub, nstate=nstate),
        grid=(seq // (tt * nsub),),
        in_specs=[pl.BlockSpec((bsz, tt * nsub, width), lambda i: (0, i, 0))] + [_const_spec(c.shape) for c in consts],
        out_specs=pl.BlockSpec((bsz, tt * nsub, dm2 // 2), lambda i: (0, i, 0)),
        out_shape=jax.ShapeDtypeStruct((bsz, seq, dm2 // 2), BF16),
        scratch_shapes=[pltpu.VMEM((nsub, 2 * nstate // LANE, bsz * tt, LANE), F32),
                        pltpu.VMEM((nstate // LANE, bsz, LANE), F32),
                        pltpu.VMEM((nstate // LANE, bsz, LANE), F32)],
        compiler_params=_params("arbitrary"),
        name="s5_scan",
    )(u3, *consts)


def _attn_kernel(q_ref, k_ref, vt_ref, o_ref, s_scr, p_scr, acc_scr, acc_new_scr, *, tq, nh):
    qi = pl.program_id(2)
    heads = [slice(h * LANE, (h + 1) * LANE) for h in range(nh)]
    acc_scr[...] = jnp.zeros_like(acc_scr)

    def scores(blk, h, masked=False, nblk=1):
        rows = pl.ds(pl.multiple_of(blk * tq, tq), nblk * tq)
        s = _dot_nt(k_ref[rows, heads[h]], q_ref[:, heads[h]])
        if masked:
            keep = (lax.broadcasted_iota(jnp.int32, (tq, tq), 0) <= lax.broadcasted_iota(jnp.int32, (tq, tq), 1))
            s = jnp.where(keep, s, NEG_INF)
        return s

    def values(blk, h):
        return vt_ref[blk, h * VT_ROWS:(h + 1) * VT_ROWS, :]

    def two_pass_chunk(blk, ms, masked):
        for h in range(min(QK_AHEAD, nh)):
            s_scr[h] = scores(blk, h, masked)
        new_ms = []
        for h in range(nh):
            if h + QK_AHEAD < nh:
                s_scr[h + QK_AHEAD] = scores(blk, h + QK_AHEAD, masked)
            tiles = [s_scr[h, r * F32_SUBLANES:(r + 1) * F32_SUBLANES, :] for r in range(tq // F32_SUBLANES)]
            part = tiles[:MAX_CHAINS]
            for r, t in enumerate(tiles[MAX_CHAINS:]):
                part[r % MAX_CHAINS] = jnp.maximum(part[r % MAX_CHAINS], t)
            m8 = functools.reduce(jnp.maximum, part)
            m_new = jnp.maximum(ms[h], jnp.max(m8, axis=0, keepdims=True))
            for r in range(tq // BF16_SUBLANES):
                sl = slice(r * BF16_SUBLANES, (r + 1) * BF16_SUBLANES)
                p_scr[h, sl, :] = jnp.exp2(s_scr[h, sl, :] - m_new).astype(BF16)
            acc_scr[h] = jnp.exp2(ms[h] - m_new) * acc_scr[h] + _dot(values(blk, h), p_scr[h, :tq, :])
            new_ms.append(m_new)
        return tuple(new_ms)

    def single_pass(blk, nblk, ms):
        excess = []

        def softmax(h):
            s = scores(blk, h, nblk=nblk)
            p_scr[h, :nblk * tq, :] = jnp.exp2(s - ms[h]).astype(BF16)
            excess.append(jnp.max(s, axis=0, keepdims=True) - ms[h])

        def accumulate(h):
            acc = acc_scr[h]
            for i in range(nblk):
                acc = acc + _dot(values(blk + i, h), p_scr[h, i * tq:(i + 1) * tq, :])
            acc_new_scr[h] = acc

        softmax(0)
        for h in range(1, nh):
            softmax(h)
            accumulate(h - 1)
        accumulate(nh - 1)
        return jnp.max(functools.reduce(jnp.maximum, excess))

    def step(blk, nblk, ms):
        worst = single_pass(blk, nblk, ms)

        def redo(ms):
            for i in range(nblk):
                ms = two_pass_chunk(blk + i, ms, False)
            return ms

        def commit(ms):
            acc_scr[...] = acc_new_scr[...]
            return ms

        return lax.cond(worst > REF_MARGIN, redo, commit, ms)

    ms = (jnp.full((1, tq), NEG_INF, F32),) * nh
    ms = two_pass_chunk(qi, ms, True)
    nwide = qi // FAST_CHUNKS
    ms = lax.fori_loop(0, nwide, lambda t, c: step(t * FAST_CHUNKS, FAST_CHUNKS, c), ms)
    lax.fori_loop(nwide * FAST_CHUNKS, qi, lambda j, c: step(j, 1, c), ms)
    ot = jnp.concatenate([acc_scr[h, :MLA_V_DIM, :] / acc_scr[h, MLA_V_DIM:MLA_V_DIM + 1, :] for h in range(nh)],
                         axis=0)
    o_ref[...] = ot.T.astype(BF16)


def _attention(q, k, vt, tq, nh):
    bsz, seq, _ = q.shape
    groups = MLA_HEADS // nh
    return pl.pallas_call(
        functools.partial(_attn_kernel, tq=tq, nh=nh),
        grid=(bsz, groups, seq // tq),
        in_specs=[pl.BlockSpec((None, tq, nh * LANE), lambda b, h, i: (b, i, h)),
                  pl.BlockSpec((None, seq, nh * LANE), lambda b, h, i: (b, 0, h)),
                  pl.BlockSpec((None, seq // tq, None, nh * VT_ROWS, tq), lambda b, h, i: (b, 0, h, 0, 0))],
        out_specs=pl.BlockSpec((None, tq, nh * MLA_V_DIM), lambda b, h, i: (b, i, h)),
        out_shape=jax.ShapeDtypeStruct((bsz, seq, MLA_HEADS * MLA_V_DIM), BF16),
        scratch_shapes=[pltpu.VMEM((nh, tq, tq), F32), pltpu.VMEM((nh, FAST_CHUNKS * tq, tq), BF16),
                        pltpu.VMEM((nh, VT_ROWS, tq), F32), pltpu.VMEM((nh, VT_ROWS, tq), F32)],
        compiler_params=_params("parallel", "parallel", "arbitrary"),
        name="mla_attention",
    )(q, k, vt)


def _layer_tail_kernel(x_ref, ao_ref, so_ref, gate_ref, kx_ref, vx_ref, lng_ref, lnb_ref, w_oa_ref, w_o_ref,
                       ln1g_ref, ln1b_ref, w_xq_ref, w_xo_ref, ln2g_ref, ln2b_ref, w_up_ref, w_down_ref,
                       ln3g_ref, ln3b_ref, out_ref, h2_scr, *, x_scale, chunk):
    @pl.when(pl.program_id(0) == 0)
    def _():
        h2_scr[...] = jnp.zeros_like(h2_scr)

    h2_prev = h2_scr[...]
    hb_prev = h2_prev.astype(BF16)
    nchunk = w_up_ref.shape[-1] // chunk

    def mlp_units():
        ff = jnp.zeros_like(h2_prev)
        for c in range(nchunk):
            cols = slice(c * chunk, (c + 1) * chunk)
            a = jnp.maximum(_dot(hb_prev, w_up_ref[:, cols]), 0.0)
            a = (a * a).astype(BF16)
            yield None
            ff = ff + _dot(a, w_down_ref[cols, :])
            yield ff

    units = mlp_units()
    ff = [None]

    def issue(n):
        for _ in range(n):
            ff[0] = next(units, ff[0])

    h0 = _layer_norm(x_ref[...], lng_ref[...], lnb_ref[...])
    d = h0.shape[-1]
    a_out = _dot(ao_ref[...], w_oa_ref[...])
    issue(1)
    gate = gate_ref[...]
    mixed = gate[:, :d].astype(F32) * so_ref[...].astype(F32) + gate[:, d:].astype(F32) * a_out
    mix = _dot(mixed.astype(BF16), w_o_ref[...])
    issue(2)
    h1 = _layer_norm(DN_ALPHA * h0 + mix, ln1g_ref[...], ln1b_ref[...])
    qx = (_dot(h1.astype(BF16), w_xq_ref[...]) * x_scale).astype(BF16)
    issue(2)
    hd = d // XATTN_HEADS
    xa = jnp.zeros_like(h1)
    for hh in range(XATTN_HEADS):
        cols = slice(hh * hd, (hh + 1) * hd)
        s = _dot_nt(qx[:, cols], kx_ref[:, cols])
        issue(1)
        p = jnp.exp(s - jnp.max(s, axis=-1, keepdims=True))
        o = _dot(p.astype(BF16), vx_ref[:, cols]) / jnp.sum(p, axis=-1, keepdims=True)
        xa = xa + _dot(o.astype(BF16), w_xo_ref[cols, :])
    issue(2 * nchunk)
    h2 = _layer_norm(DN_ALPHA * h1 + xa, ln2g_ref[...], ln2b_ref[...])
    out_ref[...] = _layer_norm(DN_ALPHA * h2_prev + ff[0], ln3g_ref[...], ln3b_ref[...])
    h2_scr[...] = h2


def _layer_tail(x, ao, so, gate, kx, vx, ln_g, ln_b, w_oa, w_o, ln1_g, ln1_b, w_xq, w_xo, ln2_g, ln2_b,
                w_up, w_down, ln3_g, ln3_b, tm):
    bsz, seq, d = x.shape
    m = kx.shape[1]
    hidden = w_up.shape[-1]
    per_seq = seq // tm
    ntile = bsz * per_seq
    vec = lambda a: a.reshape(1, d)
    flat = lambda a: a.reshape(bsz * seq, a.shape[-1])
    cur = lambda i: jnp.minimum(i, ntile - 1)
    tok = lambda w: pl.BlockSpec((tm, w), lambda i: (cur(i), 0))
    per_b = pl.BlockSpec((None, m, d), lambda i: (cur(i) // per_seq, 0, 0))
    consts = [vec(ln_g), vec(ln_b), w_oa.astype(BF16), w_o.astype(BF16), vec(ln1_g), vec(ln1_b),
              w_xq.astype(BF16), w_xo.astype(BF16), vec(ln2_g), vec(ln2_b),
              w_up.astype(BF16), w_down.astype(BF16), vec(ln3_g), vec(ln3_b)]
    out = pl.pallas_call(
        functools.partial(_layer_tail_kernel, x_scale=(d // XATTN_HEADS) ** -0.5, chunk=min(hidden, MLP_CHUNK)),
        grid=(ntile + 1,),
        in_specs=[tok(d), tok(ao.shape[-1]), tok(d), tok(2 * d), per_b, per_b] + [_const_spec(c.shape) for c in consts],
        out_specs=pl.BlockSpec((tm, d), lambda i: (jnp.maximum(i - 1, 0), 0)),
        out_shape=jax.ShapeDtypeStruct((bsz * seq, d), F32),
        scratch_shapes=[pltpu.VMEM((tm, d), F32)],
        compiler_params=_params("arbitrary"),
        name="layer_tail",
    )(flat(x), flat(ao), flat(so), flat(gate), kx, vx, *consts)
    return out.reshape(bsz, seq, d)


def kernel(x, mem, positions, ln_in_g, ln_in_b, w_in, s5_lam_re, s5_lam_im, s5_log_dt, s5_b_re, s5_b_im,
           s5_c_re, s5_c_im, s5_d, w_glu, q_norm_g, w_uq, kv_norm_g, w_ukv, w_oa, w_o, ln1_g, ln1_b,
           w_xq, w_xk, w_xv, w_xo, ln2_g, ln2_b, w_up, w_down, ln3_g, ln3_b):
    bsz, seq, d = x.shape
    n = bsz * seq
    assert w_in.shape[0] == DEPTH == 1
    tm = min(seq, 512)
    tq = min(seq, 512)
    nh = 4
    tt = min(seq, 128)
    nsub = 2

    cos, sin = _rope_tables(positions)
    ones = jnp.ones((bsz, seq, MLA_NOPE_DIM), F32)
    zeros = jnp.zeros((bsz, seq, LANE - MLA_QK_DIM), F32)
    cos128 = jnp.concatenate([ones, cos, cos, zeros], axis=-1)
    sin128 = jnp.concatenate([0.0 * ones, sin, sin, zeros], axis=-1)

    assert tm == tq
    u, q, k, vt, gate = _in_proj(x, cos128, sin128, ln_in_g, ln_in_b, w_in[0],
                                 q_norm_g[0], w_uq[0], kv_norm_g[0], w_ukv[0], tm, nh)

    ab_re, ab_im, bb_re, bb_im = _s5_discretize(s5_lam_re[0], s5_lam_im[0], s5_log_dt[0], s5_b_re[0], s5_b_im[0])
    s_out = _s5_scan(u, ab_re, ab_im, bb_re, bb_im, s5_c_re[0], s5_c_im[0],
                     s5_d[0], w_glu[0], tt, nsub)

    a_o = _attention(q, k, vt, tq, nh)

    kx, vx = _mem_kv(mem, w_xk[0], w_xv[0])
    return _layer_tail(x, a_o, s_out, gate, kx, vx, ln_in_g, ln_in_b, w_oa[0], w_o[0], ln1_g[0], ln1_b[0],
                       w_xq[0], w_xo[0], ln2_g[0], ln2_b[0], w_up[0], w_down[0], ln3_g[0], ln3_b[0], tm)
```

```python
import functools
import math

import jax
import jax.numpy as jnp
from jax import lax
from jax.experimental import pallas as pl
from jax.experimental.pallas import tpu as pltpu

F32 = jnp.float32
BF16 = jnp.bfloat16

S5_GROUP_CH = 16
S5_STATE = 64
S5_MAX_RE = -1e-4
MLA_HEADS = 8
MLA_NOPE_DIM = 64
MLA_ROPE_DIM = 32
MLA_QK_DIM = MLA_NOPE_DIM + MLA_ROPE_DIM
MLA_V_DIM = 64
ROPE_THETA = 10000.0
XATTN_HEADS = 4
LN_EPS = 1e-5
RMS_EPS = 1e-6
NEG_INF = -1e30
DEPTH = 1
DN_ALPHA = (2.0 * DEPTH) ** 0.25

MAX_CHAINS = 4
MLP_CHUNK = 1024
QK_AHEAD = 2
FAST_CHUNKS = 4
REF_MARGIN = 64.0
F32_SUBLANES = 8
BF16_SUBLANES = 16
VT_ROWS = MLA_V_DIM + BF16_SUBLANES
LANE = 128
VMEM_LIMIT = 56 * 1024 * 1024


def _const_spec(shape):
    nd = len(shape)
    return pl.BlockSpec(shape, lambda *_: (0,) * nd, pipeline_mode=pl.Buffered(1))


def _params(*sem, flags=None):
    return pltpu.CompilerParams(dimension_semantics=sem, vmem_limit_bytes=VMEM_LIMIT, flags=flags)


def _layer_norm(x, g, b):
    mu = jnp.mean(x, axis=-1, keepdims=True)
    xc = x - mu
    var = jnp.mean(xc * xc, axis=-1, keepdims=True)
    return xc * lax.rsqrt(var + LN_EPS) * g + b


def _rms_norm(x, g):
    return x * lax.rsqrt(jnp.mean(x * x, axis=-1, keepdims=True) + RMS_EPS) * g


def _gelu_exact(x):
    return 0.5 * x * (1.0 + lax.erf(x * (0.5 ** 0.5)))


def _dot(a, b):
    return jnp.dot(a, b, preferred_element_type=F32)


def _dot_nt(a, b):
    return lax.dot_general(a, b, (((1,), (1,)), ((), ())), preferred_element_type=F32)


def _s5_disc_kernel(lr_ref, li_ref, ldt_ref, br_ref, bi_ref, are_ref, aim_ref, bbr_ref, bbi_ref):
    lr = jnp.minimum(lr_ref[...], S5_MAX_RE)
    li = li_ref[...]
    dt = jnp.exp(ldt_ref[...])
    mag = jnp.exp(lr * dt)
    ang = li * dt
    ab_re = mag * jnp.cos(ang)
    ab_im = mag * jnp.sin(ang)
    den = lr * lr + li * li
    nr = ab_re - 1.0
    f_re = (nr * lr + ab_im * li) / den
    f_im = (ab_im * lr - nr * li) / den
    br = br_ref[...]
    bi = bi_ref[...]
    are_ref[...] = ab_re
    aim_ref[...] = ab_im
    bbr_ref[...] = f_re * br - f_im * bi
    bbi_ref[...] = f_re * bi + f_im * br


def _s5_discretize(lam_re, lam_im, log_dt, b_re, b_im):
    g, p = lam_re.shape
    h = b_re.shape[-1]
    n = g * p
    col = lambda a: a.reshape(n, 1)
    ldt = jnp.broadcast_to(log_dt[:, None], (g, p))
    outs = pl.pallas_call(
        _s5_disc_kernel,
        out_shape=[jax.ShapeDtypeStruct((n, 1), F32), jax.ShapeDtypeStruct((n, 1), F32),
                   jax.ShapeDtypeStruct((n, h), F32), jax.ShapeDtypeStruct((n, h), F32)],
        name="s5_discretize",
    )(col(lam_re), col(lam_im), col(ldt), b_re.reshape(n, h), b_im.reshape(n, h))
    ab_re, ab_im, bb_re, bb_im = outs
    return ab_re.reshape(g, p), ab_im.reshape(g, p), bb_re.reshape(g, p, h), bb_im.reshape(g, p, h)


def _rope_kernel(pos_ref, inv_ref, cos_ref, sin_ref):
    ang = pos_ref[...].astype(F32) * inv_ref[...]
    cos_ref[...] = jnp.cos(ang)
    sin_ref[...] = jnp.sin(ang)


def _rope_tables(positions):
    bsz, seq = positions.shape
    half = MLA_ROPE_DIM // 2
    per_row = LANE // half
    rows = bsz * seq // per_row
    inv = ROPE_THETA ** (-jnp.arange(0, MLA_ROPE_DIM, 2, dtype=F32) / MLA_ROPE_DIM)
    inv_row = jnp.tile(inv, per_row).reshape(1, LANE)
    pos_rep = jnp.repeat(positions.reshape(-1), half).reshape(rows, LANE)
    tr = min(rows, 1024)
    cos, sin = pl.pallas_call(
        _rope_kernel,
        grid=(rows // tr,),
        in_specs=[pl.BlockSpec((tr, LANE), lambda i: (i, 0)), _const_spec((1, LANE))],
        out_specs=[pl.BlockSpec((tr, LANE), lambda i: (i, 0))] * 2,
        out_shape=[jax.ShapeDtypeStruct((rows, LANE), F32)] * 2,
        compiler_params=_params("parallel"),
        name="rope_table",
    )(pos_rep, inv_row)
    return cos.reshape(bsz, seq, half), sin.reshape(bsz, seq, half)


def _mem_kv_kernel(mem_ref, wk_ref, wv_ref, k_ref, v_ref):
    m = mem_ref[...].astype(BF16)
    k_ref[...] = _dot(m, wk_ref[...]).astype(BF16)
    v_ref[...] = _dot(m, wv_ref[...]).astype(BF16)


def _mem_kv(mem, w_xk, w_xv):
    bsz, m, d = mem.shape
    return pl.pallas_call(
        _mem_kv_kernel,
        grid=(bsz,),
        in_specs=[pl.BlockSpec((None, m, d), lambda b: (b, 0, 0)), _const_spec((d, d)), _const_spec((d, d))],
        out_specs=[pl.BlockSpec((None, m, d), lambda b: (b, 0, 0))] * 2,
        out_shape=[jax.ShapeDtypeStruct((bsz, m, d), BF16)] * 2,
        compiler_params=_params("parallel"),
        name="mem_kv",
    )(mem, w_xk.astype(BF16), w_xv.astype(BF16))


def _in_proj_kernel(x_ref, cos_ref, sin_ref, lng_ref, lnb_ref, w_u_ref, w_cq_ref, w_ckv_ref, w_kr_ref,
                    w_g_ref, qg_ref, kvg_ref, w_q_ref, w_qs_ref, w_k_ref, w_vt_ref,
                    u_ref, q_ref, k_ref, vt_ref, gate_ref, *, q_scale, nh):
    hb = _layer_norm(x_ref[...], lng_ref[...], lnb_ref[...]).astype(BF16)
    cos = cos_ref[...]
    sin = sin_ref[...]
    gate_cols = gate_ref.shape[-1] // 4

    def gate_quarter(c):
        cols = slice(c * gate_cols, (c + 1) * gate_cols)
        gate_ref[:, cols] = jax.nn.sigmoid(_dot(hb, w_g_ref[:, cols])).astype(BF16)

    cq = _dot(hb, w_cq_ref[...])
    ckv = _dot(hb, w_ckv_ref[...])
    gate_quarter(0)
    cqn = _rms_norm(cq, qg_ref[...]).astype(BF16)
    qa = _dot(cqn, w_q_ref[...])
    qb = _dot(cqn, w_qs_ref[...])
    gate_quarter(1)
    ckvn = _rms_norm(ckv, kvg_ref[...]).astype(BF16)
    kn = _dot(ckvn, w_k_ref[...])
    kr2 = _dot(hb, w_kr_ref[...])
    k_rope = kr2[:, :LANE] * cos + kr2[:, LANE:] * sin
    vt = _dot_nt(w_vt_ref[...], ckvn)
    gate_quarter(2)
    gate_quarter(3)
    u_ref[...] = _dot(hb, w_u_ref[...])
    ones = jnp.ones((VT_ROWS - MLA_V_DIM, vt.shape[-1]), BF16)
    for h in range(MLA_HEADS):
        g, r0 = h // nh, (h % nh) * VT_ROWS
        vt_ref[g, r0:r0 + MLA_V_DIM, :] = vt[h * MLA_V_DIM:(h + 1) * MLA_V_DIM, :].astype(BF16)
        vt_ref[g, r0 + MLA_V_DIM:r0 + VT_ROWS, :] = ones
    for h in range(MLA_HEADS):
        sl = slice(h * LANE, (h + 1) * LANE)
        q_ref[:, sl] = ((qa[:, sl] * cos + qb[:, sl] * sin) * q_scale).astype(BF16)
        k_ref[:, sl] = (kn[:, sl] + k_rope).astype(BF16)


def _rotate_half_cols(w):
    half = MLA_ROPE_DIM // 2
    return jnp.concatenate([-w[..., half:], w[..., :half]], axis=-1)


def _head_slab(nope, rope):
    d, h, _ = nope.shape
    pad = jnp.zeros((d, h, LANE - MLA_QK_DIM), nope.dtype)
    return jnp.concatenate([nope, rope, pad], axis=-1).reshape(d, h * LANE)


def _in_proj(x, cos128, sin128, ln_g, ln_b, w_in, q_norm_g, w_uq, kv_norm_g, w_ukv, tm, nh):
    bsz, seq, d = x.shape
    groups = MLA_HEADS // nh
    s5w = q_rank = kv_rank = d // 4
    o1, o2, o3 = s5w, s5w + q_rank, s5w + q_rank + kv_rank
    o4 = o3 + MLA_ROPE_DIM
    hl = MLA_HEADS * LANE
    w_u = w_in[:, :o1].astype(BF16)
    w_cq = w_in[:, o1:o2].astype(BF16)
    w_ckv = w_in[:, o2:o3].astype(BF16)
    w_kr = w_in[:, o3:o4]
    zl = jnp.zeros((d, MLA_NOPE_DIM), F32)
    zr = jnp.zeros((d, LANE - MLA_QK_DIM), F32)
    w_kr2 = jnp.concatenate([zl, w_kr, zr, zl, _rotate_half_cols(w_kr), zr], axis=-1).astype(BF16)
    w_g = w_in[:, o4:].astype(BF16)
    wq3 = w_uq.reshape(q_rank, MLA_HEADS, MLA_QK_DIM)
    wq_nope, wq_rope = wq3[..., :MLA_NOPE_DIM], wq3[..., MLA_NOPE_DIM:]
    w_q = _head_slab(wq_nope, wq_rope).astype(BF16)
    w_qs = _head_slab(jnp.zeros_like(wq_nope), _rotate_half_cols(wq_rope)).astype(BF16)
    wkv3 = w_ukv.reshape(kv_rank, MLA_HEADS, MLA_NOPE_DIM + MLA_V_DIM)
    w_k = _head_slab(wkv3[..., :MLA_NOPE_DIM],
                     jnp.zeros((kv_rank, MLA_HEADS, MLA_ROPE_DIM), F32)).astype(BF16)
    w_vt = wkv3[..., MLA_NOPE_DIM:].reshape(kv_rank, MLA_HEADS * MLA_V_DIM).T.astype(BF16)
    row = lambda w: pl.BlockSpec((None, tm, w), lambda b, i: (b, i, 0))
    tok = lambda w, dt: jax.ShapeDtypeStruct((bsz, seq, w), dt)
    consts = [ln_g.reshape(1, d), ln_b.reshape(1, d), w_u, w_cq, w_ckv, w_kr2, w_g,
              q_norm_g.reshape(1, q_rank), kv_norm_g.reshape(1, kv_rank), w_q, w_qs, w_k, w_vt]
    return pl.pallas_call(
        functools.partial(_in_proj_kernel, q_scale=MLA_QK_DIM ** -0.5 * math.log2(math.e), nh=nh),
        grid=(bsz, seq // tm),
        in_specs=[row(d), row(LANE), row(LANE)] + [_const_spec(c.shape) for c in consts],
        out_specs=[row(s5w), row(hl), row(hl),
                   pl.BlockSpec((None, None, groups, nh * VT_ROWS, tm), lambda b, i: (b, i, 0, 0, 0)), row(2 * d)],
        out_shape=[tok(s5w, F32), tok(hl, BF16), tok(hl, BF16),
                   jax.ShapeDtypeStruct((bsz, seq // tm, groups, nh * VT_ROWS, tm), BF16), tok(2 * d, BF16)],
        compiler_params=_params("parallel", "parallel"),
        name="in_proj",
    )(x, cos128, sin128, *consts)


def _s5_scan_kernel(u_ref, bblk_ref, are_ref, aim_ref, cblk_ref, d_ref, wglu_ref, out_ref,
                    hbuf, st_re, st_im, *, bsz, tt, nsub, nstate):
    @pl.when(pl.program_id(0) == 0)
    def _():
        st_re[...] = jnp.zeros_like(st_re)
        st_im[...] = jnp.zeros_like(st_im)

    width = u_ref.shape[-1]
    nt = nstate // LANE
    ar = [jnp.broadcast_to(are_ref[:, c * LANE:(c + 1) * LANE], (bsz, LANE)) for c in range(nt)]
    ai = [jnp.broadcast_to(aim_ref[:, c * LANE:(c + 1) * LANE], (bsz, LANE)) for c in range(nt)]

    def project_in(sc):
        u = jnp.swapaxes(u_ref[:, sc * tt:(sc + 1) * tt, :], 0, 1).reshape(tt * bsz, width)
        bu = _dot(u.astype(BF16), bblk_ref[...])
        for c in range(2 * nt):
            hbuf[sc, c] = bu[:, c * LANE:(c + 1) * LANE]
        return u

    def recurrence(sc, hr, hi):
        for t in range(tt):
            rows = slice(t * bsz, (t + 1) * bsz)
            nr, ni = [], []
            for c in range(nt):
                r = ar[c] * hr[c] - ai[c] * hi[c] + hbuf[sc, c, rows, :]
                i = ar[c] * hi[c] + ai[c] * hr[c] + hbuf[sc, nt + c, rows, :]
                hbuf[sc, c, rows, :] = r
                hbuf[sc, nt + c, rows, :] = i
                nr.append(r)
                ni.append(i)
            hr, hi = nr, ni
        return hr, hi

    def project_out(sc, u):
        half = tt * bsz // 2
        ys = []
        for r in range(2):
            rows = slice(r * half, (r + 1) * half)
            hs = jnp.concatenate([hbuf[sc, c, rows, :].astype(BF16) for c in range(2 * nt)], axis=-1)
            ys.append(_dot(hs, cblk_ref[...]))
        y = jnp.concatenate(ys, axis=0) + d_ref[...] * u
        y = jnp.swapaxes(y.reshape(tt, bsz, width), 0, 1).reshape(bsz * tt, width)
        z = _dot(_gelu_exact(y).astype(BF16), wglu_ref[...])
        dm = z.shape[-1] // 2
        s_out = z[:, :dm] * jax.nn.sigmoid(z[:, dm:])
        out_ref[:, sc * tt:(sc + 1) * tt, :] = s_out.reshape(bsz, tt, dm).astype(BF16)

    us = [project_in(sc) for sc in range(nsub)]
    hr = [st_re[c] for c in range(nt)]
    hi = [st_im[c] for c in range(nt)]
    for sc in range(nsub):
        hr, hi = recurrence(sc, hr, hi)
        project_out(sc, us[sc])
    for c in range(nt):
        st_re[c] = hr[c]
        st_im[c] = hi[c]


def _s5_scan(u3, ab_re, ab_im, bb_re, bb_im, c_re, c_im, d_skip, w_glu, tt, nsub):
    bsz, seq, width = u3.shape
    g, p, h = bb_re.shape
    nstate = g * p
    eye = jnp.eye(g, dtype=F32)
    blk_b = lambda bb: jnp.einsum('gph,gk->ghkp', bb, eye).reshape(g * h, nstate)
    bblk = jnp.concatenate([blk_b(bb_re), blk_b(bb_im)], axis=1).astype(BF16)
    blk_c = lambda c: jnp.einsum('ghp,gk->gpkh', c, eye).reshape(nstate, g * h)
    cblk = jnp.concatenate([blk_c(c_re), -blk_c(c_im)], axis=0).astype(BF16)
    dm2 = w_glu.shape[-1]
    consts = [bblk, ab_re.reshape(1, nstate), ab_im.reshape(1, nstate), cblk,
              d_skip.reshape(1, width), w_glu.astype(BF16)]
    return pl.pallas_call(
        functools.partial(_s5_scan_kernel, bsz=bsz, tt=tt, nsub=nsub, nstate=nstate),
        grid=(seq // (tt * nsub),),
        in_specs=[pl.BlockSpec((bsz, tt * nsub, width), lambda i: (0, i, 0))] + [_const_spec(c.shape) for c in consts],
        out_specs=pl.BlockSpec((bsz, tt * nsub, dm2 // 2), lambda i: (0, i, 0)),
        out_shape=jax.ShapeDtypeStruct((bsz, seq, dm2 // 2), BF16),
        scratch_shapes=[pltpu.VMEM((nsub, 2 * nstate // LANE, bsz * tt, LANE), F32),
                        pltpu.VMEM((nstate // LANE, bsz, LANE), F32),
                        pltpu.VMEM((nstate // LANE, bsz, LANE), F32)],
        compiler_params=_params("arbitrary"),
        name="s5_scan",
    )(u3, *consts)


def _attn_kernel(q_ref, k_ref, vt_ref, o_ref, s_scr, p_scr, acc_scr, acc_new_scr, *, tq, nh):
    qi = pl.program_id(2)
    heads = [slice(h * LANE, (h + 1) * LANE) for h in range(nh)]
    acc_scr[...] = jnp.zeros_like(acc_scr)

    def scores(blk, h, masked=False, nblk=1):
        rows = pl.ds(pl.multiple_of(blk * tq, tq), nblk * tq)
        s = _dot_nt(k_ref[rows, heads[h]], q_ref[:, heads[h]])
        if masked:
            keep = (lax.broadcasted_iota(jnp.int32, (tq, tq), 0) <= lax.broadcasted_iota(jnp.int32, (tq, tq), 1))
            s = jnp.where(keep, s, NEG_INF)
        return s

    def values(blk, h):
        return vt_ref[blk, h * VT_ROWS:(h + 1) * VT_ROWS, :]

    def two_pass_chunk(blk, ms, masked):
        for h in range(min(QK_AHEAD, nh)):
            s_scr[h] = scores(blk, h, masked)
        new_ms = []
        for h in range(nh):
            if h + QK_AHEAD < nh:
                s_scr[h + QK_AHEAD] = scores(blk, h + QK_AHEAD, masked)
            tiles = [s_scr[h, r * F32_SUBLANES:(r + 1) * F32_SUBLANES, :] for r in range(tq // F32_SUBLANES)]
            part = tiles[:MAX_CHAINS]
            for r, t in enumerate(tiles[MAX_CHAINS:]):
                part[r % MAX_CHAINS] = jnp.maximum(part[r % MAX_CHAINS], t)
            m8 = functools.reduce(jnp.maximum, part)
            m_new = jnp.maximum(ms[h], jnp.max(m8, axis=0, keepdims=True))
            for r in range(tq // BF16_SUBLANES):
                sl = slice(r * BF16_SUBLANES, (r + 1) * BF16_SUBLANES)
                p_scr[h, sl, :] = jnp.exp2(s_scr[h, sl, :] - m_new).astype(BF16)
            acc_scr[h] = jnp.exp2(ms[h] - m_new) * acc_scr[h] + _dot(values(blk, h), p_scr[h, :tq, :])
            new_ms.append(m_new)
        return tuple(new_ms)

    def single_pass(blk, nblk, ms):
        excess = []

        def softmax(h):
            s = scores(blk, h, nblk=nblk)
            p_scr[h, :nblk * tq, :] = jnp.exp2(s - ms[h]).astype(BF16)
            excess.append(jnp.max(s, axis=0, keepdims=True) - ms[h])

        def accumulate(h):
            acc = acc_scr[h]
            for i in range(nblk):
                acc = acc + _dot(values(blk + i, h), p_scr[h, i * tq:(i + 1) * tq, :])
            acc_new_scr[h] = acc

        softmax(0)
        for h in range(1, nh):
            softmax(h)
            accumulate(h - 1)
        accumulate(nh - 1)
        return jnp.max(functools.reduce(jnp.maximum, excess))

    def step(blk, nblk, ms):
        worst = single_pass(blk, nblk, ms)

        def redo(ms):
            return lax.fori_loop(0, nblk, lambda i, c: two_pass_chunk(blk + i, c, False), ms)

        def commit(ms):
            acc_scr[...] = acc_new_scr[...]
            return ms

        return lax.cond(worst > REF_MARGIN, redo, commit, ms)

    ms = (jnp.full((1, tq), NEG_INF, F32),) * nh
    ms = two_pass_chunk(qi, ms, True)
    done, width = 0, FAST_CHUNKS
    while width >= 1:
        trips = (qi - done) // width
        ms = lax.fori_loop(0, trips, lambda t, c, done=done, width=width: step(done + t * width, width, c), ms)
        done, width = done + trips * width, width // 2
    ot = jnp.concatenate([acc_scr[h, :MLA_V_DIM, :] / acc_scr[h, MLA_V_DIM:MLA_V_DIM + 1, :] for h in range(nh)],
                         axis=0)
    o_ref[...] = ot.T.astype(BF16)


def _attention(q, k, vt, tq, nh):
    bsz, seq, _ = q.shape
    groups = MLA_HEADS // nh
    return pl.pallas_call(
        functools.partial(_attn_kernel, tq=tq, nh=nh),
        grid=(bsz, groups, seq // tq),
        in_specs=[pl.BlockSpec((None, tq, nh * LANE), lambda b, h, i: (b, i, h)),
                  pl.BlockSpec((None, seq, nh * LANE), lambda b, h, i: (b, 0, h)),
                  pl.BlockSpec((None, seq // tq, None, nh * VT_ROWS, tq), lambda b, h, i: (b, 0, h, 0, 0))],
        out_specs=pl.BlockSpec((None, tq, nh * MLA_V_DIM), lambda b, h, i: (b, i, h)),
        out_shape=jax.ShapeDtypeStruct((bsz, seq, MLA_HEADS * MLA_V_DIM), BF16),
        scratch_shapes=[pltpu.VMEM((nh, tq, tq), F32), pltpu.VMEM((nh, FAST_CHUNKS * tq, tq), BF16),
                        pltpu.VMEM((nh, VT_ROWS, tq), F32), pltpu.VMEM((nh, VT_ROWS, tq), F32)],
        compiler_params=_params("parallel", "parallel", "arbitrary"),
        name="mla_attention",
    )(q, k, vt)


def _layer_tail_kernel(x_ref, ao_ref, so_ref, gate_ref, kx_ref, vx_ref, lng_ref, lnb_ref, w_oa_ref, w_o_ref,
                       ln1g_ref, ln1b_ref, w_xq_ref, w_xo_ref, ln2g_ref, ln2b_ref, w_up_ref, w_down_ref,
                       ln3g_ref, ln3b_ref, out_ref, h2_scr, *, x_scale, chunk):
    @pl.when(pl.program_id(0) == 0)
    def _():
        h2_scr[...] = jnp.zeros_like(h2_scr)

    h2_prev = h2_scr[...]
    hb_prev = h2_prev.astype(BF16)
    nchunk = w_up_ref.shape[-1] // chunk

    def mlp_units():
        ff = jnp.zeros_like(h2_prev)
        for c in range(nchunk):
            cols = slice(c * chunk, (c + 1) * chunk)
            a = jnp.maximum(_dot(hb_prev, w_up_ref[:, cols]), 0.0)
            a = (a * a).astype(BF16)
            yield None
            ff = ff + _dot(a, w_down_ref[cols, :])
            yield ff

    units = mlp_units()
    ff = [None]

    def issue(n):
        for _ in range(n):
            ff[0] = next(units, ff[0])

    h0 = _layer_norm(x_ref[...], lng_ref[...], lnb_ref[...])
    d = h0.shape[-1]
    a_out = _dot(ao_ref[...], w_oa_ref[...])
    issue(1)
    gate = gate_ref[...]
    mixed = gate[:, :d].astype(F32) * so_ref[...].astype(F32) + gate[:, d:].astype(F32) * a_out
    mix = _dot(mixed.astype(BF16), w_o_ref[...])
    issue(2)
    h1 = _layer_norm(DN_ALPHA * h0 + mix, ln1g_ref[...], ln1b_ref[...])
    qx = (_dot(h1.astype(BF16), w_xq_ref[...]) * x_scale).astype(BF16)
    issue(2)
    hd = d // XATTN_HEADS
    xa = jnp.zeros_like(h1)
    for hh in range(XATTN_HEADS):
        cols = slice(hh * hd, (hh + 1) * hd)
        s = _dot_nt(qx[:, cols], kx_ref[:, cols])
        issue(1)
        p = jnp.exp(s - jnp.max(s, axis=-1, keepdims=True))
        o = _dot(p.astype(BF16), vx_ref[:, cols]) / jnp.sum(p, axis=-1, keepdims=True)
        xa = xa + _dot(o.astype(BF16), w_xo_ref[cols, :])
    issue(2 * nchunk)
    h2 = _layer_norm(DN_ALPHA * h1 + xa, ln2g_ref[...], ln2b_ref[...])
    out_ref[...] = _layer_norm(DN_ALPHA * h2_prev + ff[0], ln3g_ref[...], ln3b_ref[...])
    h2_scr[...] = h2


def _layer_tail(x, ao, so, gate, kx, vx, ln_g, ln_b, w_oa, w_o, ln1_g, ln1_b, w_xq, w_xo, ln2_g, ln2_b,
                w_up, w_down, ln3_g, ln3_b, tm):
    bsz, seq, d = x.shape
    m = kx.shape[1]
    hidden = w_up.shape[-1]
    per_seq = seq // tm
    ntile = bsz * per_seq
    vec = lambda a: a.reshape(1, d)
    flat = lambda a: a.reshape(bsz * seq, a.shape[-1])
    cur = lambda i: jnp.minimum(i, ntile - 1)
    tok = lambda w: pl.BlockSpec((tm, w), lambda i: (cur(i), 0))
    per_b = pl.BlockSpec((None, m, d), lambda i: (cur(i) // per_seq, 0, 0))
    consts = [vec(ln_g), vec(ln_b), w_oa.astype(BF16), w_o.astype(BF16), vec(ln1_g), vec(ln1_b),
              w_xq.astype(BF16), w_xo.astype(BF16), vec(ln2_g), vec(ln2_b),
              w_up.astype(BF16), w_down.astype(BF16), vec(ln3_g), vec(ln3_b)]
    out = pl.pallas_call(
        functools.partial(_layer_tail_kernel, x_scale=(d // XATTN_HEADS) ** -0.5, chunk=min(hidden, MLP_CHUNK)),
        grid=(ntile + 1,),
        in_specs=[tok(d), tok(ao.shape[-1]), tok(d), tok(2 * d), per_b, per_b] + [_const_spec(c.shape) for c in consts],
        out_specs=pl.BlockSpec((tm, d), lambda i: (jnp.maximum(i - 1, 0), 0)),
        out_shape=jax.ShapeDtypeStruct((bsz * seq, d), F32),
        scratch_shapes=[pltpu.VMEM((tm, d), F32)],
        compiler_params=_params("arbitrary"),
        name="layer_tail",
    )(flat(x), flat(ao), flat(so), flat(gate), kx, vx, *consts)
    return out.reshape(bsz, seq, d)


def kernel(x, mem, positions, ln_in_g, ln_in_b, w_in, s5_lam_re, s5_lam_im, s5_log_dt, s5_b_re, s5_b_im,
           s5_c_re, s5_c_im, s5_d, w_glu, q_norm_g, w_uq, kv_norm_g, w_ukv, w_oa, w_o, ln1_g, ln1_b,
           w_xq, w_xk, w_xv, w_xo, ln2_g, ln2_b, w_up, w_down, ln3_g, ln3_b):
    bsz, seq, d = x.shape
    n = bsz * seq
    assert w_in.shape[0] == DEPTH == 1
    tm = min(seq, 512)
    tq = min(seq, 512)
    nh = 4
    tt = min(seq, 128)
    nsub = 2

    cos, sin = _rope_tables(positions)
    ones = jnp.ones((bsz, seq, MLA_NOPE_DIM), F32)
    zeros = jnp.zeros((bsz, seq, LANE - MLA_QK_DIM), F32)
    cos128 = jnp.concatenate([ones, cos, cos, zeros], axis=-1)
    sin128 = jnp.concatenate([0.0 * ones, sin, sin, zeros], axis=-1)

    assert tm == tq
    u, q, k, vt, gate = _in_proj(x, cos128, sin128, ln_in_g, ln_in_b, w_in[0],
                                 q_norm_g[0], w_uq[0], kv_norm_g[0], w_ukv[0], tm, nh)

    ab_re, ab_im, bb_re, bb_im = _s5_discretize(s5_lam_re[0], s5_lam_im[0], s5_log_dt[0], s5_b_re[0], s5_b_im[0])
    s_out = _s5_scan(u, ab_re, ab_im, bb_re, bb_im, s5_c_re[0], s5_c_im[0],
                     s5_d[0], w_glu[0], tt, nsub)

    a_o = _attention(q, k, vt, tq, nh)

    kx, vx = _mem_kv(mem, w_xk[0], w_xv[0])
    return _layer_tail(x, a_o, s_out, gate, kx, vx, ln_in_g, ln_in_b, w_oa[0], w_o[0], ln1_g[0], ln1_b[0],
                       w_xq[0], w_xo[0], ln2_g[0], ln2_b[0], w_up[0], w_down[0], ln3_g[0], ln3_b[0], tm)
```

```python
import functools
import math

import jax
import jax.numpy as jnp
from jax import lax
from jax.experimental import pallas as pl
from jax.experimental.pallas import tpu as pltpu

F32 = jnp.float32
BF16 = jnp.bfloat16

S5_GROUP_CH = 16
S5_STATE = 64
S5_MAX_RE = -1e-4
MLA_HEADS = 8
MLA_NOPE_DIM = 64
MLA_ROPE_DIM = 32
MLA_QK_DIM = MLA_NOPE_DIM + MLA_ROPE_DIM
MLA_V_DIM = 64
ROPE_THETA = 10000.0
XATTN_HEADS = 4
LN_EPS = 1e-5
RMS_EPS = 1e-6
NEG_INF = -1e30
DEPTH = 1
DN_ALPHA = (2.0 * DEPTH) ** 0.25

MAX_CHAINS = 4
MLP_CHUNK = 1024
QK_AHEAD = 2
FAST_CHUNKS = 8
P_SLOTS = 2
REF_MARGIN = 64.0
F32_SUBLANES = 8
BF16_SUBLANES = 16
VT_ROWS = MLA_V_DIM + BF16_SUBLANES
LANE = 128
VMEM_LIMIT = 56 * 1024 * 1024


def _const_spec(shape):
    nd = len(shape)
    return pl.BlockSpec(shape, lambda *_: (0,) * nd, pipeline_mode=pl.Buffered(1))


def _params(*sem, flags=None):
    return pltpu.CompilerParams(dimension_semantics=sem, vmem_limit_bytes=VMEM_LIMIT, flags=flags)


def _layer_norm(x, g, b):
    mu = jnp.mean(x, axis=-1, keepdims=True)
    xc = x - mu
    var = jnp.mean(xc * xc, axis=-1, keepdims=True)
    return xc * lax.rsqrt(var + LN_EPS) * g + b


def _rms_norm(x, g):
    return x * lax.rsqrt(jnp.mean(x * x, axis=-1, keepdims=True) + RMS_EPS) * g


def _gelu_exact(x):
    return 0.5 * x * (1.0 + lax.erf(x * (0.5 ** 0.5)))


def _dot(a, b):
    return jnp.dot(a, b, preferred_element_type=F32)


def _dot_nt(a, b):
    return lax.dot_general(a, b, (((1,), (1,)), ((), ())), preferred_element_type=F32)


def _s5_disc_kernel(lr_ref, li_ref, ldt_ref, br_ref, bi_ref, are_ref, aim_ref, bbr_ref, bbi_ref):
    lr = jnp.minimum(lr_ref[...], S5_MAX_RE)
    li = li_ref[...]
    dt = jnp.exp(ldt_ref[...])
    mag = jnp.exp(lr * dt)
    ang = li * dt
    ab_re = mag * jnp.cos(ang)
    ab_im = mag * jnp.sin(ang)
    den = lr * lr + li * li
    nr = ab_re - 1.0
    f_re = (nr * lr + ab_im * li) / den
    f_im = (ab_im * lr - nr * li) / den
    br = br_ref[...]
    bi = bi_ref[...]
    are_ref[...] = ab_re
    aim_ref[...] = ab_im
    bbr_ref[...] = f_re * br - f_im * bi
    bbi_ref[...] = f_re * bi + f_im * br


def _s5_discretize(lam_re, lam_im, log_dt, b_re, b_im):
    g, p = lam_re.shape
    h = b_re.shape[-1]
    n = g * p
    col = lambda a: a.reshape(n, 1)
    ldt = jnp.broadcast_to(log_dt[:, None], (g, p))
    outs = pl.pallas_call(
        _s5_disc_kernel,
        out_shape=[jax.ShapeDtypeStruct((n, 1), F32), jax.ShapeDtypeStruct((n, 1), F32),
                   jax.ShapeDtypeStruct((n, h), F32), jax.ShapeDtypeStruct((n, h), F32)],
        name="s5_discretize",
    )(col(lam_re), col(lam_im), col(ldt), b_re.reshape(n, h), b_im.reshape(n, h))
    ab_re, ab_im, bb_re, bb_im = outs
    return ab_re.reshape(g, p), ab_im.reshape(g, p), bb_re.reshape(g, p, h), bb_im.reshape(g, p, h)


def _rope_kernel(pos_ref, inv_ref, cos_ref, sin_ref):
    ang = pos_ref[...].astype(F32) * inv_ref[...]
    cos_ref[...] = jnp.cos(ang)
    sin_ref[...] = jnp.sin(ang)


def _rope_tables(positions):
    bsz, seq = positions.shape
    half = MLA_ROPE_DIM // 2
    per_row = LANE // half
    rows = bsz * seq // per_row
    inv = ROPE_THETA ** (-jnp.arange(0, MLA_ROPE_DIM, 2, dtype=F32) / MLA_ROPE_DIM)
    inv_row = jnp.tile(inv, per_row).reshape(1, LANE)
    pos_rep = jnp.repeat(positions.reshape(-1), half).reshape(rows, LANE)
    tr = min(rows, 1024)
    cos, sin = pl.pallas_call(
        _rope_kernel,
        grid=(rows // tr,),
        in_specs=[pl.BlockSpec((tr, LANE), lambda i: (i, 0)), _const_spec((1, LANE))],
        out_specs=[pl.BlockSpec((tr, LANE), lambda i: (i, 0))] * 2,
        out_shape=[jax.ShapeDtypeStruct((rows, LANE), F32)] * 2,
        compiler_params=_params("parallel"),
        name="rope_table",
    )(pos_rep, inv_row)
    return cos.reshape(bsz, seq, half), sin.reshape(bsz, seq, half)


def _mem_kv_kernel(mem_ref, wk_ref, wv_ref, k_ref, v_ref):
    m = mem_ref[...].astype(BF16)
    k_ref[...] = _dot(m, wk_ref[...]).astype(BF16)
    v_ref[...] = _dot(m, wv_ref[...]).astype(BF16)


def _mem_kv(mem, w_xk, w_xv):
    bsz, m, d = mem.shape
    return pl.pallas_call(
        _mem_kv_kernel,
        grid=(bsz,),
        in_specs=[pl.BlockSpec((None, m, d), lambda b: (b, 0, 0)), _const_spec((d, d)), _const_spec((d, d))],
        out_specs=[pl.BlockSpec((None, m, d), lambda b: (b, 0, 0))] * 2,
        out_shape=[jax.ShapeDtypeStruct((bsz, m, d), BF16)] * 2,
        compiler_params=_params("parallel"),
        name="mem_kv",
    )(mem, w_xk.astype(BF16), w_xv.astype(BF16))


def _in_proj_kernel(x_ref, cos_ref, sin_ref, lng_ref, lnb_ref, w_u_ref, w_cq_ref, w_ckv_ref, w_kr_ref,
                    w_g_ref, qg_ref, kvg_ref, w_q_ref, w_qs_ref, w_k_ref, w_vt_ref,
                    u_ref, q_ref, k_ref, vt_ref, gate_ref, *, q_scale, nh):
    hb = _layer_norm(x_ref[...], lng_ref[...], lnb_ref[...]).astype(BF16)
    cos = cos_ref[...]
    sin = sin_ref[...]
    gate_cols = gate_ref.shape[-1] // 4

    def gate_quarter(c):
        cols = slice(c * gate_cols, (c + 1) * gate_cols)
        gate_ref[:, cols] = jax.nn.sigmoid(_dot(hb, w_g_ref[:, cols])).astype(BF16)

    cq = _dot(hb, w_cq_ref[...])
    ckv = _dot(hb, w_ckv_ref[...])
    gate_quarter(0)
    cqn = _rms_norm(cq, qg_ref[...]).astype(BF16)
    qa = _dot(cqn, w_q_ref[...])
    qb = _dot(cqn, w_qs_ref[...])
    gate_quarter(1)
    ckvn = _rms_norm(ckv, kvg_ref[...]).astype(BF16)
    kn = _dot(ckvn, w_k_ref[...])
    kr2 = _dot(hb, w_kr_ref[...])
    k_rope = kr2[:, :LANE] * cos + kr2[:, LANE:] * sin
    vt = _dot_nt(w_vt_ref[...], ckvn)
    gate_quarter(2)
    gate_quarter(3)
    u_ref[...] = _dot(hb, w_u_ref[...])
    ones = jnp.ones((VT_ROWS - MLA_V_DIM, vt.shape[-1]), BF16)
    for h in range(MLA_HEADS):
        g, r0 = h // nh, (h % nh) * VT_ROWS
        vt_ref[g, r0:r0 + MLA_V_DIM, :] = vt[h * MLA_V_DIM:(h + 1) * MLA_V_DIM, :].astype(BF16)
        vt_ref[g, r0 + MLA_V_DIM:r0 + VT_ROWS, :] = ones
    for h in range(MLA_HEADS):
        sl = slice(h * LANE, (h + 1) * LANE)
        q_ref[:, sl] = ((qa[:, sl] * cos + qb[:, sl] * sin) * q_scale).astype(BF16)
        k_ref[:, sl] = (kn[:, sl] + k_rope).astype(BF16)


def _rotate_half_cols(w):
    half = MLA_ROPE_DIM // 2
    return jnp.concatenate([-w[..., half:], w[..., :half]], axis=-1)


def _head_slab(nope, rope):
    d, h, _ = nope.shape
    pad = jnp.zeros((d, h, LANE - MLA_QK_DIM), nope.dtype)
    return jnp.concatenate([nope, rope, pad], axis=-1).reshape(d, h * LANE)


def _in_proj(x, cos128, sin128, ln_g, ln_b, w_in, q_norm_g, w_uq, kv_norm_g, w_ukv, tm, nh):
    bsz, seq, d = x.shape
    groups = MLA_HEADS // nh
    s5w = q_rank = kv_rank = d // 4
    o1, o2, o3 = s5w, s5w + q_rank, s5w + q_rank + kv_rank
    o4 = o3 + MLA_ROPE_DIM
    hl = MLA_HEADS * LANE
    w_u = w_in[:, :o1].astype(BF16)
    w_cq = w_in[:, o1:o2].astype(BF16)
    w_ckv = w_in[:, o2:o3].astype(BF16)
    w_kr = w_in[:, o3:o4]
    zl = jnp.zeros((d, MLA_NOPE_DIM), F32)
    zr = jnp.zeros((d, LANE - MLA_QK_DIM), F32)
    w_kr2 = jnp.concatenate([zl, w_kr, zr, zl, _rotate_half_cols(w_kr), zr], axis=-1).astype(BF16)
    w_g = w_in[:, o4:].astype(BF16)
    wq3 = w_uq.reshape(q_rank, MLA_HEADS, MLA_QK_DIM)
    wq_nope, wq_rope = wq3[..., :MLA_NOPE_DIM], wq3[..., MLA_NOPE_DIM:]
    w_q = _head_slab(wq_nope, wq_rope).astype(BF16)
    w_qs = _head_slab(jnp.zeros_like(wq_nope), _rotate_half_cols(wq_rope)).astype(BF16)
    wkv3 = w_ukv.reshape(kv_rank, MLA_HEADS, MLA_NOPE_DIM + MLA_V_DIM)
    w_k = _head_slab(wkv3[..., :MLA_NOPE_DIM],
                     jnp.zeros((kv_rank, MLA_HEADS, MLA_ROPE_DIM), F32)).astype(BF16)
    w_vt = wkv3[..., MLA_NOPE_DIM:].reshape(kv_rank, MLA_HEADS * MLA_V_DIM).T.astype(BF16)
    row = lambda w: pl.BlockSpec((None, tm, w), lambda b, i: (b, i, 0))
    tok = lambda w, dt: jax.ShapeDtypeStruct((bsz, seq, w), dt)
    consts = [ln_g.reshape(1, d), ln_b.reshape(1, d), w_u, w_cq, w_ckv, w_kr2, w_g,
              q_norm_g.reshape(1, q_rank), kv_norm_g.reshape(1, kv_rank), w_q, w_qs, w_k, w_vt]
    return pl.pallas_call(
        functools.partial(_in_proj_kernel, q_scale=MLA_QK_DIM ** -0.5 * math.log2(math.e), nh=nh),
        grid=(bsz, seq // tm),
        in_specs=[row(d), row(LANE), row(LANE)] + [_const_spec(c.shape) for c in consts],
        out_specs=[row(s5w), row(hl), row(hl),
                   pl.BlockSpec((None, None, groups, nh * VT_ROWS, tm), lambda b, i: (b, i, 0, 0, 0)), row(2 * d)],
        out_shape=[tok(s5w, F32), tok(hl, BF16), tok(hl, BF16),
                   jax.ShapeDtypeStruct((bsz, seq // tm, groups, nh * VT_ROWS, tm), BF16), tok(2 * d, BF16)],
        compiler_params=_params("parallel", "parallel"),
        name="in_proj",
    )(x, cos128, sin128, *consts)


def _s5_scan_kernel(u_ref, bblk_ref, are_ref, aim_ref, cblk_ref, d_ref, wglu_ref, out_ref,
                    hbuf, st_re, st_im, *, bsz, tt, nsub, nstate):
    @pl.when(pl.program_id(0) == 0)
    def _():
        st_re[...] = jnp.zeros_like(st_re)
        st_im[...] = jnp.zeros_like(st_im)

    width = u_ref.shape[-1]
    nt = nstate // LANE
    ar = [jnp.broadcast_to(are_ref[:, c * LANE:(c + 1) * LANE], (bsz, LANE)) for c in range(nt)]
    ai = [jnp.broadcast_to(aim_ref[:, c * LANE:(c + 1) * LANE], (bsz, LANE)) for c in range(nt)]

    def project_in(sc):
        u = jnp.swapaxes(u_ref[:, sc * tt:(sc + 1) * tt, :], 0, 1).reshape(tt * bsz, width)
        bu = _dot(u.astype(BF16), bblk_ref[...])
        for c in range(2 * nt):
            hbuf[sc, c] = bu[:, c * LANE:(c + 1) * LANE]
        return u

    def recurrence(sc, hr, hi):
        for t in range(tt):
            rows = slice(t * bsz, (t + 1) * bsz)
            nr, ni = [], []
            for c in range(nt):
                r = ar[c] * hr[c] - ai[c] * hi[c] + hbuf[sc, c, rows, :]
                i = ar[c] * hi[c] + ai[c] * hr[c] + hbuf[sc, nt + c, rows, :]
                hbuf[sc, c, rows, :] = r
                hbuf[sc, nt + c, rows, :] = i
                nr.append(r)
                ni.append(i)
            hr, hi = nr, ni
        return hr, hi

    def project_out(sc, u):
        half = tt * bsz // 2
        ys = []
        for r in range(2):
            rows = slice(r * half, (r + 1) * half)
            hs = jnp.concatenate([hbuf[sc, c, rows, :].astype(BF16) for c in range(2 * nt)], axis=-1)
            ys.append(_dot(hs, cblk_ref[...]))
        y = jnp.concatenate(ys, axis=0) + d_ref[...] * u
        y = jnp.swapaxes(y.reshape(tt, bsz, width), 0, 1).reshape(bsz * tt, width)
        z = _dot(_gelu_exact(y).astype(BF16), wglu_ref[...])
        dm = z.shape[-1] // 2
        s_out = z[:, :dm] * jax.nn.sigmoid(z[:, dm:])
        out_ref[:, sc * tt:(sc + 1) * tt, :] = s_out.reshape(bsz, tt, dm).astype(BF16)

    us = [project_in(sc) for sc in range(nsub)]
    hr = [st_re[c] for c in range(nt)]
    hi = [st_im[c] for c in range(nt)]
    for sc in range(nsub):
        hr, hi = recurrence(sc, hr, hi)
        project_out(sc, us[sc])
    for c in range(nt):
        st_re[c] = hr[c]
        st_im[c] = hi[c]


def _s5_scan(u3, ab_re, ab_im, bb_re, bb_im, c_re, c_im, d_skip, w_glu, tt, nsub):
    bsz, seq, width = u3.shape
    g, p, h = bb_re.shape
    nstate = g * p
    eye = jnp.eye(g, dtype=F32)
    blk_b = lambda bb: jnp.einsum('gph,gk->ghkp', bb, eye).reshape(g * h, nstate)
    bblk = jnp.concatenate([blk_b(bb_re), blk_b(bb_im)], axis=1).astype(BF16)
    blk_c = lambda c: jnp.einsum('ghp,gk->gpkh', c, eye).reshape(nstate, g * h)
    cblk = jnp.concatenate([blk_c(c_re), -blk_c(c_im)], axis=0).astype(BF16)
    dm2 = w_glu.shape[-1]
    consts = [bblk, ab_re.reshape(1, nstate), ab_im.reshape(1, nstate), cblk,
              d_skip.reshape(1, width), w_glu.astype(BF16)]
    return pl.pallas_call(
        functools.partial(_s5_scan_kernel, bsz=bsz, tt=tt, nsub=nsub, nstate=nstate),
        grid=(seq // (tt * nsub),),
        in_specs=[pl.BlockSpec((bsz, tt * nsub, width), lambda i: (0, i, 0))] + [_const_spec(c.shape) for c in consts],
        out_specs=pl.BlockSpec((bsz, tt * nsub, dm2 // 2), lambda i: (0, i, 0)),
        out_shape=jax.ShapeDtypeStruct((bsz, seq, dm2 // 2), BF16),
        scratch_shapes=[pltpu.VMEM((nsub, 2 * nstate // LANE, bsz * tt, LANE), F32),
                        pltpu.VMEM((nstate // LANE, bsz, LANE), F32),
                        pltpu.VMEM((nstate // LANE, bsz, LANE), F32)],
        compiler_params=_params("arbitrary"),
        name="s5_scan",
    )(u3, *consts)


def _attn_kernel(q_ref, k_ref, vt_ref, o_ref, s_scr, p_scr, acc_scr, acc_new_scr, *, tq, nh):
    qi = pl.program_id(2)
    heads = [slice(h * LANE, (h + 1) * LANE) for h in range(nh)]
    acc_scr[...] = jnp.zeros_like(acc_scr)

    def scores(blk, h, masked=False, nblk=1):
        rows = pl.ds(pl.multiple_of(blk * tq, tq), nblk * tq)
        s = _dot_nt(k_ref[rows, heads[h]], q_ref[:, heads[h]])
        if masked:
            keep = (lax.broadcasted_iota(jnp.int32, (tq, tq), 0) <= lax.broadcasted_iota(jnp.int32, (tq, tq), 1))
            s = jnp.where(keep, s, NEG_INF)
        return s

    def values(blk, h):
        return vt_ref[blk, h * VT_ROWS:(h + 1) * VT_ROWS, :]

    def two_pass_chunk(blk, ms, masked):
        for h in range(min(QK_AHEAD, nh)):
            s_scr[h] = scores(blk, h, masked)
        new_ms = []
        for h in range(nh):
            if h + QK_AHEAD < nh:
                s_scr[h + QK_AHEAD] = scores(blk, h + QK_AHEAD, masked)
            tiles = [s_scr[h, r * F32_SUBLANES:(r + 1) * F32_SUBLANES, :] for r in range(tq // F32_SUBLANES)]
            part = tiles[:MAX_CHAINS]
            for r, t in enumerate(tiles[MAX_CHAINS:]):
                part[r % MAX_CHAINS] = jnp.maximum(part[r % MAX_CHAINS], t)
            m8 = functools.reduce(jnp.maximum, part)
            m_new = jnp.maximum(ms[h], jnp.max(m8, axis=0, keepdims=True))
            for r in range(tq // BF16_SUBLANES):
                sl = slice(r * BF16_SUBLANES, (r + 1) * BF16_SUBLANES)
                p_scr[h % P_SLOTS, sl, :] = jnp.exp2(s_scr[h, sl, :] - m_new).astype(BF16)
            acc_scr[h] = jnp.exp2(ms[h] - m_new) * acc_scr[h] + _dot(values(blk, h), p_scr[h % P_SLOTS, :tq, :])
            new_ms.append(m_new)
        return tuple(new_ms)

    def single_pass(blk, nblk, ms):
        excess = []

        def softmax(h):
            s = scores(blk, h, nblk=nblk)
            p_scr[h % P_SLOTS, :nblk * tq, :] = jnp.exp2(s - ms[h]).astype(BF16)
            excess.append(jnp.max(s, axis=0, keepdims=True) - ms[h])

        def accumulate(h):
            acc = acc_scr[h]
            for i in range(nblk):
                acc = acc + _dot(values(blk + i, h), p_scr[h % P_SLOTS, i * tq:(i + 1) * tq, :])
            acc_new_scr[h] = acc

        softmax(0)
        for h in range(1, nh):
            softmax(h)
            accumulate(h - 1)
        accumulate(nh - 1)
        return jnp.max(functools.reduce(jnp.maximum, excess))

    def step(blk, nblk, ms):
        worst = single_pass(blk, nblk, ms)

        def redo(ms):
            return lax.fori_loop(0, nblk, lambda i, c: two_pass_chunk(blk + i, c, False), ms)

        def commit(ms):
            acc_scr[...] = acc_new_scr[...]
            return ms

        return lax.cond(worst > REF_MARGIN, redo, commit, ms)

    ms = (jnp.full((1, tq), NEG_INF, F32),) * nh
    ms = two_pass_chunk(qi, ms, True)
    done, width = 0, FAST_CHUNKS
    while width >= 1:
        trips = (qi - done) // width
        ms = lax.fori_loop(0, trips, lambda t, c, done=done, width=width: step(done + t * width, width, c), ms)
        done, width = done + trips * width, width // 2
    ot = jnp.concatenate([acc_scr[h, :MLA_V_DIM, :] / acc_scr[h, MLA_V_DIM:MLA_V_DIM + 1, :] for h in range(nh)],
                         axis=0)
    o_ref[...] = ot.T.astype(BF16)


def _attention(q, k, vt, tq, nh):
    bsz, seq, _ = q.shape
    groups = MLA_HEADS // nh
    return pl.pallas_call(
        functools.partial(_attn_kernel, tq=tq, nh=nh),
        grid=(bsz, groups, seq // tq),
        in_specs=[pl.BlockSpec((None, tq, nh * LANE), lambda b, h, i: (b, i, h)),
                  pl.BlockSpec((None, seq, nh * LANE), lambda b, h, i: (b, 0, h)),
                  pl.BlockSpec((None, seq // tq, None, nh * VT_ROWS, tq), lambda b, h, i: (b, 0, h, 0, 0))],
        out_specs=pl.BlockSpec((None, tq, nh * MLA_V_DIM), lambda b, h, i: (b, i, h)),
        out_shape=jax.ShapeDtypeStruct((bsz, seq, MLA_HEADS * MLA_V_DIM), BF16),
        scratch_shapes=[pltpu.VMEM((nh, tq, tq), F32), pltpu.VMEM((P_SLOTS, FAST_CHUNKS * tq, tq), BF16),
                        pltpu.VMEM((nh, VT_ROWS, tq), F32), pltpu.VMEM((nh, VT_ROWS, tq), F32)],
        compiler_params=_params("parallel", "parallel", "arbitrary"),
        name="mla_attention",
    )(q, k, vt)


def _layer_tail_kernel(x_ref, ao_ref, so_ref, gate_ref, kx_ref, vx_ref, lng_ref, lnb_ref, w_oa_ref, w_o_ref,
                       ln1g_ref, ln1b_ref, w_xq_ref, w_xo_ref, ln2g_ref, ln2b_ref, w_up_ref, w_down_ref,
                       ln3g_ref, ln3b_ref, out_ref, h2_scr, *, x_scale, chunk):
    @pl.when(pl.program_id(0) == 0)
    def _():
        h2_scr[...] = jnp.zeros_like(h2_scr)

    h2_prev = h2_scr[...]
    hb_prev = h2_prev.astype(BF16)
    nchunk = w_up_ref.shape[-1] // chunk

    def mlp_units():
        ff = jnp.zeros_like(h2_prev)
        for c in range(nchunk):
            cols = slice(c * chunk, (c + 1) * chunk)
            a = jnp.maximum(_dot(hb_prev, w_up_ref[:, cols]), 0.0)
            a = (a * a).astype(BF16)
            yield None
            ff = ff + _dot(a, w_down_ref[cols, :])
            yield ff

    units = mlp_units()
    ff = [None]

    def issue(n):
        for _ in range(n):
            ff[0] = next(units, ff[0])

    h0 = _layer_norm(x_ref[...], lng_ref[...], lnb_ref[...])
    d = h0.shape[-1]
    a_out = _dot(ao_ref[...], w_oa_ref[...])
    issue(1)
    gate = gate_ref[...]
    mixed = gate[:, :d].astype(F32) * so_ref[...].astype(F32) + gate[:, d:].astype(F32) * a_out
    mix = _dot(mixed.astype(BF16), w_o_ref[...])
    issue(2)
    h1 = _layer_norm(DN_ALPHA * h0 + mix, ln1g_ref[...], ln1b_ref[...])
    qx = (_dot(h1.astype(BF16), w_xq_ref[...]) * x_scale).astype(BF16)
    issue(2)
    hd = d // XATTN_HEADS
    xa = jnp.zeros_like(h1)
    for hh in range(XATTN_HEADS):
        cols = slice(hh * hd, (hh + 1) * hd)
        s = _dot_nt(qx[:, cols], kx_ref[:, cols])
        issue(1)
        p = jnp.exp(s - jnp.max(s, axis=-1, keepdims=True))
        o = _dot(p.astype(BF16), vx_ref[:, cols]) / jnp.sum(p, axis=-1, keepdims=True)
        xa = xa + _dot(o.astype(BF16), w_xo_ref[cols, :])
    issue(2 * nchunk)
    h2 = _layer_norm(DN_ALPHA * h1 + xa, ln2g_ref[...], ln2b_ref[...])
    out_ref[...] = _layer_norm(DN_ALPHA * h2_prev + ff[0], ln3g_ref[...], ln3b_ref[...])
    h2_scr[...] = h2


def _layer_tail(x, ao, so, gate, kx, vx, ln_g, ln_b, w_oa, w_o, ln1_g, ln1_b, w_xq, w_xo, ln2_g, ln2_b,
                w_up, w_down, ln3_g, ln3_b, tm):
    bsz, seq, d = x.shape
    m = kx.shape[1]
    hidden = w_up.shape[-1]
    per_seq = seq // tm
    ntile = bsz * per_seq
    vec = lambda a: a.reshape(1, d)
    flat = lambda a: a.reshape(bsz * seq, a.shape[-1])
    cur = lambda i: jnp.minimum(i, ntile - 1)
    tok = lambda w: pl.BlockSpec((tm, w), lambda i: (cur(i), 0))
    per_b = pl.BlockSpec((None, m, d), lambda i: (cur(i) // per_seq, 0, 0))
    consts = [vec(ln_g), vec(ln_b), w_oa.astype(BF16), w_o.astype(BF16), vec(ln1_g), vec(ln1_b),
              w_xq.astype(BF16), w_xo.astype(BF16), vec(ln2_g), vec(ln2_b),
              w_up.astype(BF16), w_down.astype(BF16), vec(ln3_g), vec(ln3_b)]
    out = pl.pallas_call(
        functools.partial(_layer_tail_kernel, x_scale=(d // XATTN_HEADS) ** -0.5, chunk=min(hidden, MLP_CHUNK)),
        grid=(ntile + 1,),
        in_specs=[tok(d), tok(ao.shape[-1]), tok(d), tok(2 * d), per_b, per_b] + [_const_spec(c.shape) for c in consts],
        out_specs=pl.BlockSpec((tm, d), lambda i: (jnp.maximum(i - 1, 0), 0)),
        out_shape=jax.ShapeDtypeStruct((bsz * seq, d), F32),
        scratch_shapes=[pltpu.VMEM((tm, d), F32)],
        compiler_params=_params("arbitrary"),
        name="layer_tail",
    )(flat(x), flat(ao), flat(so), flat(gate), kx, vx, *consts)
    return out.reshape(bsz, seq, d)


def kernel(x, mem, positions, ln_in_g, ln_in_b, w_in, s5_lam_re, s5_lam_im, s5_log_dt, s5_b_re, s5_b_im,
           s5_c_re, s5_c_im, s5_d, w_glu, q_norm_g, w_uq, kv_norm_g, w_ukv, w_oa, w_o, ln1_g, ln1_b,
           w_xq, w_xk, w_xv, w_xo, ln2_g, ln2_b, w_up, w_down, ln3_g, ln3_b):
    bsz, seq, d = x.shape
    n = bsz * seq
    assert w_in.shape[0] == DEPTH == 1
    tm = min(seq, 512)
    tq = min(seq, 512)
    nh = 4
    tt = min(seq, 128)
    nsub = 2

    cos, sin = _rope_tables(positions)
    ones = jnp.ones((bsz, seq, MLA_NOPE_DIM), F32)
    zeros = jnp.zeros((bsz, seq, LANE - MLA_QK_DIM), F32)
    cos128 = jnp.concatenate([ones, cos, cos, zeros], axis=-1)
    sin128 = jnp.concatenate([0.0 * ones, sin, sin, zeros], axis=-1)

    assert tm == tq
    u, q, k, vt, gate = _in_proj(x, cos128, sin128, ln_in_g, ln_in_b, w_in[0],
                                 q_norm_g[0], w_uq[0], kv_norm_g[0], w_ukv[0], tm, nh)

    ab_re, ab_im, bb_re, bb_im = _s5_discretize(s5_lam_re[0], s5_lam_im[0], s5_log_dt[0], s5_b_re[0], s5_b_im[0])
    s_out = _s5_scan(u, ab_re, ab_im, bb_re, bb_im, s5_c_re[0], s5_c_im[0],
                     s5_d[0], w_glu[0], tt, nsub)

    a_o = _attention(q, k, vt, tq, nh)

    kx, vx = _mem_kv(mem, w_xk[0], w_xv[0])
    return _layer_tail(x, a_o, s_out, gate, kx, vx, ln_in_g, ln_in_b, w_oa[0], w_o[0], ln1_g[0], ln1_b[0],
                       w_xq[0], w_xo[0], ln2_g[0], ln2_b[0], w_up[0], w_down[0], ln3_g[0], ln3_b[0], tm)
```

```python
import functools
import math

import jax
import jax.numpy as jnp
from jax import lax
from jax.experimental import pallas as pl
from jax.experimental.pallas import tpu as pltpu

F32 = jnp.float32
BF16 = jnp.bfloat16

S5_GROUP_CH = 16
S5_STATE = 64
S5_MAX_RE = -1e-4
MLA_HEADS = 8
MLA_NOPE_DIM = 64
MLA_ROPE_DIM = 32
MLA_QK_DIM = MLA_NOPE_DIM + MLA_ROPE_DIM
MLA_V_DIM = 64
ROPE_THETA = 10000.0
XATTN_HEADS = 4
LN_EPS = 1e-5
RMS_EPS = 1e-6
NEG_INF = -1e30
DEPTH = 1
DN_ALPHA = (2.0 * DEPTH) ** 0.25

MAX_CHAINS = 4
MLP_CHUNK = 1024
QK_AHEAD = 2
FAST_CHUNKS = 8
P_SLOTS = 2
REF_MARGIN = 64.0
F32_SUBLANES = 8
BF16_SUBLANES = 16
VT_ROWS = MLA_V_DIM + BF16_SUBLANES
LANE = 128
VMEM_LIMIT = 56 * 1024 * 1024


def _const_spec(shape):
    nd = len(shape)
    return pl.BlockSpec(shape, lambda *_: (0,) * nd, pipeline_mode=pl.Buffered(1))


def _params(*sem, flags=None):
    return pltpu.CompilerParams(dimension_semantics=sem, vmem_limit_bytes=VMEM_LIMIT, flags=flags)


def _layer_norm(x, g, b):
    mu = jnp.mean(x, axis=-1, keepdims=True)
    xc = x - mu
    var = jnp.mean(xc * xc, axis=-1, keepdims=True)
    return xc * lax.rsqrt(var + LN_EPS) * g + b


def _rms_norm(x, g):
    return x * lax.rsqrt(jnp.mean(x * x, axis=-1, keepdims=True) + RMS_EPS) * g


def _gelu_exact(x):
    return 0.5 * x * (1.0 + lax.erf(x * (0.5 ** 0.5)))


def _dot(a, b):
    return jnp.dot(a, b, preferred_element_type=F32)


def _dot_nt(a, b):
    return lax.dot_general(a, b, (((1,), (1,)), ((), ())), preferred_element_type=F32)


def _s5_disc_kernel(lr_ref, li_ref, ldt_ref, br_ref, bi_ref, are_ref, aim_ref, bbr_ref, bbi_ref):
    lr = jnp.minimum(lr_ref[...], S5_MAX_RE)
    li = li_ref[...]
    dt = jnp.exp(ldt_ref[...])
    mag = jnp.exp(lr * dt)
    ang = li * dt
    ab_re = mag * jnp.cos(ang)
    ab_im = mag * jnp.sin(ang)
    den = lr * lr + li * li
    nr = ab_re - 1.0
    f_re = (nr * lr + ab_im * li) / den
    f_im = (ab_im * lr - nr * li) / den
    br = br_ref[...]
    bi = bi_ref[...]
    are_ref[...] = ab_re
    aim_ref[...] = ab_im
    bbr_ref[...] = f_re * br - f_im * bi
    bbi_ref[...] = f_re * bi + f_im * br


def _s5_discretize(lam_re, lam_im, log_dt, b_re, b_im):
    g, p = lam_re.shape
    h = b_re.shape[-1]
    n = g * p
    col = lambda a: a.reshape(n, 1)
    ldt = jnp.broadcast_to(log_dt[:, None], (g, p))
    outs = pl.pallas_call(
        _s5_disc_kernel,
        out_shape=[jax.ShapeDtypeStruct((n, 1), F32), jax.ShapeDtypeStruct((n, 1), F32),
                   jax.ShapeDtypeStruct((n, h), F32), jax.ShapeDtypeStruct((n, h), F32)],
        name="s5_discretize",
    )(col(lam_re), col(lam_im), col(ldt), b_re.reshape(n, h), b_im.reshape(n, h))
    ab_re, ab_im, bb_re, bb_im = outs
    return ab_re.reshape(g, p), ab_im.reshape(g, p), bb_re.reshape(g, p, h), bb_im.reshape(g, p, h)


def _rope_kernel(pos_ref, inv_ref, cos_ref, sin_ref):
    ang = pos_ref[...].astype(F32) * inv_ref[...]
    cos_ref[...] = jnp.cos(ang)
    sin_ref[...] = jnp.sin(ang)


def _rope_tables(positions):
    bsz, seq = positions.shape
    half = MLA_ROPE_DIM // 2
    per_row = LANE // half
    rows = bsz * seq // per_row
    inv = ROPE_THETA ** (-jnp.arange(0, MLA_ROPE_DIM, 2, dtype=F32) / MLA_ROPE_DIM)
    inv_row = jnp.tile(inv, per_row).reshape(1, LANE)
    pos_rep = jnp.repeat(positions.reshape(-1), half).reshape(rows, LANE)
    tr = min(rows, 1024)
    cos, sin = pl.pallas_call(
        _rope_kernel,
        grid=(rows // tr,),
        in_specs=[pl.BlockSpec((tr, LANE), lambda i: (i, 0)), _const_spec((1, LANE))],
        out_specs=[pl.BlockSpec((tr, LANE), lambda i: (i, 0))] * 2,
        out_shape=[jax.ShapeDtypeStruct((rows, LANE), F32)] * 2,
        compiler_params=_params("parallel"),
        name="rope_table",
    )(pos_rep, inv_row)
    return cos.reshape(bsz, seq, half), sin.reshape(bsz, seq, half)


def _mem_kv_kernel(mem_ref, wk_ref, wv_ref, k_ref, v_ref):
    m = mem_ref[...].astype(BF16)
    k_ref[...] = _dot(m, wk_ref[...]).astype(BF16)
    v_ref[...] = _dot(m, wv_ref[...]).astype(BF16)


def _mem_kv(mem, w_xk, w_xv):
    bsz, m, d = mem.shape
    return pl.pallas_call(
        _mem_kv_kernel,
        grid=(bsz,),
        in_specs=[pl.BlockSpec((None, m, d), lambda b: (b, 0, 0)), _const_spec((d, d)), _const_spec((d, d))],
        out_specs=[pl.BlockSpec((None, m, d), lambda b: (b, 0, 0))] * 2,
        out_shape=[jax.ShapeDtypeStruct((bsz, m, d), BF16)] * 2,
        compiler_params=_params("parallel"),
        name="mem_kv",
    )(mem, w_xk.astype(BF16), w_xv.astype(BF16))


def _in_proj_kernel(x_ref, cos_ref, sin_ref, lng_ref, lnb_ref, w_u_ref, w_cq_ref, w_ckv_ref, w_kr_ref,
                    w_g_ref, qg_ref, kvg_ref, w_q_ref, w_qs_ref, w_k_ref, w_vt_ref,
                    u_ref, q_ref, k_ref, vt_ref, gate_ref, *, q_scale, nh):
    hb = _layer_norm(x_ref[...], lng_ref[...], lnb_ref[...]).astype(BF16)
    cos = cos_ref[...]
    sin = sin_ref[...]
    gate_cols = gate_ref.shape[-1] // 4

    def gate_quarter(c):
        cols = slice(c * gate_cols, (c + 1) * gate_cols)
        gate_ref[:, cols] = jax.nn.sigmoid(_dot(hb, w_g_ref[:, cols])).astype(BF16)

    cq = _dot(hb, w_cq_ref[...])
    ckv = _dot(hb, w_ckv_ref[...])
    gate_quarter(0)
    cqn = _rms_norm(cq, qg_ref[...]).astype(BF16)
    qa = _dot(cqn, w_q_ref[...])
    qb = _dot(cqn, w_qs_ref[...])
    gate_quarter(1)
    ckvn = _rms_norm(ckv, kvg_ref[...]).astype(BF16)
    kn = _dot(ckvn, w_k_ref[...])
    kr2 = _dot(hb, w_kr_ref[...])
    k_rope = kr2[:, :LANE] * cos + kr2[:, LANE:] * sin
    vt = _dot_nt(w_vt_ref[...], ckvn)
    gate_quarter(2)
    gate_quarter(3)
    u_ref[...] = _dot(hb, w_u_ref[...])
    ones = jnp.ones((VT_ROWS - MLA_V_DIM, vt.shape[-1]), BF16)
    for h in range(MLA_HEADS):
        g, r0 = h // nh, (h % nh) * VT_ROWS
        vt_ref[g, r0:r0 + MLA_V_DIM, :] = vt[h * MLA_V_DIM:(h + 1) * MLA_V_DIM, :].astype(BF16)
        vt_ref[g, r0 + MLA_V_DIM:r0 + VT_ROWS, :] = ones
    for h in range(MLA_HEADS):
        sl = slice(h * LANE, (h + 1) * LANE)
        q_ref[:, sl] = ((qa[:, sl] * cos + qb[:, sl] * sin) * q_scale).astype(BF16)
        k_ref[:, sl] = (kn[:, sl] + k_rope).astype(BF16)


def _rotate_half_cols(w):
    half = MLA_ROPE_DIM // 2
    return jnp.concatenate([-w[..., half:], w[..., :half]], axis=-1)


def _head_slab(nope, rope):
    d, h, _ = nope.shape
    pad = jnp.zeros((d, h, LANE - MLA_QK_DIM), nope.dtype)
    return jnp.concatenate([nope, rope, pad], axis=-1).reshape(d, h * LANE)


def _in_proj(x, cos128, sin128, ln_g, ln_b, w_in, q_norm_g, w_uq, kv_norm_g, w_ukv, tm, nh):
    bsz, seq, d = x.shape
    groups = MLA_HEADS // nh
    s5w = q_rank = kv_rank = d // 4
    o1, o2, o3 = s5w, s5w + q_rank, s5w + q_rank + kv_rank
    o4 = o3 + MLA_ROPE_DIM
    hl = MLA_HEADS * LANE
    w_u = w_in[:, :o1].astype(BF16)
    w_cq = w_in[:, o1:o2].astype(BF16)
    w_ckv = w_in[:, o2:o3].astype(BF16)
    w_kr = w_in[:, o3:o4]
    zl = jnp.zeros((d, MLA_NOPE_DIM), F32)
    zr = jnp.zeros((d, LANE - MLA_QK_DIM), F32)
    w_kr2 = jnp.concatenate([zl, w_kr, zr, zl, _rotate_half_cols(w_kr), zr], axis=-1).astype(BF16)
    w_g = w_in[:, o4:].astype(BF16)
    wq3 = w_uq.reshape(q_rank, MLA_HEADS, MLA_QK_DIM)
    wq_nope, wq_rope = wq3[..., :MLA_NOPE_DIM], wq3[..., MLA_NOPE_DIM:]
    w_q = _head_slab(wq_nope, wq_rope).astype(BF16)
    w_qs = _head_slab(jnp.zeros_like(wq_nope), _rotate_half_cols(wq_rope)).astype(BF16)
    wkv3 = w_ukv.reshape(kv_rank, MLA_HEADS, MLA_NOPE_DIM + MLA_V_DIM)
    w_k = _head_slab(wkv3[..., :MLA_NOPE_DIM],
                     jnp.zeros((kv_rank, MLA_HEADS, MLA_ROPE_DIM), F32)).astype(BF16)
    w_vt = wkv3[..., MLA_NOPE_DIM:].reshape(kv_rank, MLA_HEADS * MLA_V_DIM).T.astype(BF16)
    row = lambda w: pl.BlockSpec((None, tm, w), lambda b, i: (b, i, 0))
    tok = lambda w, dt: jax.ShapeDtypeStruct((bsz, seq, w), dt)
    consts = [ln_g.reshape(1, d), ln_b.reshape(1, d), w_u, w_cq, w_ckv, w_kr2, w_g,
              q_norm_g.reshape(1, q_rank), kv_norm_g.reshape(1, kv_rank), w_q, w_qs, w_k, w_vt]
    return pl.pallas_call(
        functools.partial(_in_proj_kernel, q_scale=MLA_QK_DIM ** -0.5 * math.log2(math.e), nh=nh),
        grid=(bsz, seq // tm),
        in_specs=[row(d), row(LANE), row(LANE)] + [_const_spec(c.shape) for c in consts],
        out_specs=[row(s5w), row(hl), row(hl),
                   pl.BlockSpec((None, None, groups, nh * VT_ROWS, tm), lambda b, i: (b, i, 0, 0, 0)), row(2 * d)],
        out_shape=[tok(s5w, F32), tok(hl, BF16), tok(hl, BF16),
                   jax.ShapeDtypeStruct((bsz, seq // tm, groups, nh * VT_ROWS, tm), BF16), tok(2 * d, BF16)],
        compiler_params=_params("parallel", "parallel"),
        name="in_proj",
    )(x, cos128, sin128, *consts)


def _s5_scan_kernel(u_ref, bblk_ref, are_ref, aim_ref, cblk_ref, d_ref, wglu_ref, out_ref,
                    hbuf, st_re, st_im, *, bsz, tt, nsub, nstate):
    @pl.when(pl.program_id(0) == 0)
    def _():
        st_re[...] = jnp.zeros_like(st_re)
        st_im[...] = jnp.zeros_like(st_im)

    width = u_ref.shape[-1]
    nt = nstate // LANE
    ar = [jnp.broadcast_to(are_ref[:, c * LANE:(c + 1) * LANE], (bsz, LANE)) for c in range(nt)]
    ai = [jnp.broadcast_to(aim_ref[:, c * LANE:(c + 1) * LANE], (bsz, LANE)) for c in range(nt)]

    def project_in(sc):
        u = jnp.swapaxes(u_ref[:, sc * tt:(sc + 1) * tt, :], 0, 1).reshape(tt * bsz, width)
        bu = _dot(u.astype(BF16), bblk_ref[...])
        for c in range(2 * nt):
            hbuf[sc, c] = bu[:, c * LANE:(c + 1) * LANE]
        return u

    def recurrence(sc, hr, hi):
        for t in range(tt):
            rows = slice(t * bsz, (t + 1) * bsz)
            nr, ni = [], []
            for c in range(nt):
                r = ar[c] * hr[c] - ai[c] * hi[c] + hbuf[sc, c, rows, :]
                i = ar[c] * hi[c] + ai[c] * hr[c] + hbuf[sc, nt + c, rows, :]
                hbuf[sc, c, rows, :] = r
                hbuf[sc, nt + c, rows, :] = i
                nr.append(r)
                ni.append(i)
            hr, hi = nr, ni
        return hr, hi

    def project_out(sc, u):
        half = tt * bsz // 2
        ys = []
        for r in range(2):
            rows = slice(r * half, (r + 1) * half)
            hs = jnp.concatenate([hbuf[sc, c, rows, :].astype(BF16) for c in range(2 * nt)], axis=-1)
            ys.append(_dot(hs, cblk_ref[...]))
        y = jnp.concatenate(ys, axis=0) + d_ref[...] * u
        y = jnp.swapaxes(y.reshape(tt, bsz, width), 0, 1).reshape(bsz * tt, width)
        z = _dot(_gelu_exact(y).astype(BF16), wglu_ref[...])
        dm = z.shape[-1] // 2
        s_out = z[:, :dm] * jax.nn.sigmoid(z[:, dm:])
        out_ref[:, sc * tt:(sc + 1) * tt, :] = s_out.reshape(bsz, tt, dm).astype(BF16)

    us = [project_in(sc) for sc in range(nsub)]
    hr = [st_re[c] for c in range(nt)]
    hi = [st_im[c] for c in range(nt)]
    for sc in range(nsub):
        hr, hi = recurrence(sc, hr, hi)
        project_out(sc, us[sc])
    for c in range(nt):
        st_re[c] = hr[c]
        st_im[c] = hi[c]


def _s5_scan(u3, ab_re, ab_im, bb_re, bb_im, c_re, c_im, d_skip, w_glu, tt, nsub):
    bsz, seq, width = u3.shape
    g, p, h = bb_re.shape
    nstate = g * p
    eye = jnp.eye(g, dtype=F32)
    blk_b = lambda bb: jnp.einsum('gph,gk->ghkp', bb, eye).reshape(g * h, nstate)
    bblk = jnp.concatenate([blk_b(bb_re), blk_b(bb_im)], axis=1).astype(BF16)
    blk_c = lambda c: jnp.einsum('ghp,gk->gpkh', c, eye).reshape(nstate, g * h)
    cblk = jnp.concatenate([blk_c(c_re), -blk_c(c_im)], axis=0).astype(BF16)
    dm2 = w_glu.shape[-1]
    consts = [bblk, ab_re.reshape(1, nstate), ab_im.reshape(1, nstate), cblk,
              d_skip.reshape(1, width), w_glu.astype(BF16)]
    return pl.pallas_call(
        functools.partial(_s5_scan_kernel, bsz=bsz, tt=tt, nsub=nsub, nstate=nstate),
        grid=(seq // (tt * nsub),),
        in_specs=[pl.BlockSpec((bsz, tt * nsub, width), lambda i: (0, i, 0))] + [_const_spec(c.shape) for c in consts],
        out_specs=pl.BlockSpec((bsz, tt * nsub, dm2 // 2), lambda i: (0, i, 0)),
        out_shape=jax.ShapeDtypeStruct((bsz, seq, dm2 // 2), BF16),
        scratch_shapes=[pltpu.VMEM((nsub, 2 * nstate // LANE, bsz * tt, LANE), F32),
                        pltpu.VMEM((nstate // LANE, bsz, LANE), F32),
                        pltpu.VMEM((nstate // LANE, bsz, LANE), F32)],
        compiler_params=_params("arbitrary"),
        name="s5_scan",
    )(u3, *consts)


def _attn_kernel(q_ref, k_ref, vt_ref, o_ref, s_scr, p_scr, acc_scr, acc_new_scr, *, tq, nh):
    qi = pl.program_id(2)
    heads = [slice(h * LANE, (h + 1) * LANE) for h in range(nh)]
    acc_scr[...] = jnp.zeros_like(acc_scr)

    def scores(blk, h, masked=False, nblk=1):
        rows = pl.ds(pl.multiple_of(blk * tq, tq), nblk * tq)
        s = _dot_nt(k_ref[rows, heads[h]], q_ref[:, heads[h]])
        if masked:
            keep = (lax.broadcasted_iota(jnp.int32, (tq, tq), 0) <= lax.broadcasted_iota(jnp.int32, (tq, tq), 1))
            s = jnp.where(keep, s, NEG_INF)
        return s

    def values(blk, h):
        return vt_ref[blk, h * VT_ROWS:(h + 1) * VT_ROWS, :]

    def two_pass_chunk(blk, ms, masked):
        for h in range(min(QK_AHEAD, nh)):
            s_scr[h] = scores(blk, h, masked)
        new_ms = []
        for h in range(nh):
            if h + QK_AHEAD < nh:
                s_scr[h + QK_AHEAD] = scores(blk, h + QK_AHEAD, masked)
            tiles = [s_scr[h, r * F32_SUBLANES:(r + 1) * F32_SUBLANES, :] for r in range(tq // F32_SUBLANES)]
            part = tiles[:MAX_CHAINS]
            for r, t in enumerate(tiles[MAX_CHAINS:]):
                part[r % MAX_CHAINS] = jnp.maximum(part[r % MAX_CHAINS], t)
            m8 = functools.reduce(jnp.maximum, part)
            m_new = jnp.maximum(ms[h], jnp.max(m8, axis=0, keepdims=True))
            for r in range(tq // BF16_SUBLANES):
                sl = slice(r * BF16_SUBLANES, (r + 1) * BF16_SUBLANES)
                p_scr[h % P_SLOTS, sl, :] = jnp.exp2(s_scr[h, sl, :] - m_new).astype(BF16)
            acc_scr[h] = jnp.exp2(ms[h] - m_new) * acc_scr[h] + _dot(values(blk, h), p_scr[h % P_SLOTS, :tq, :])
            new_ms.append(m_new)
        return tuple(new_ms)

    def single_pass(blk, nblk, ms):
        excess = []

        def softmax(h):
            s = scores(blk, h, nblk=nblk)
            p_scr[h % P_SLOTS, :nblk * tq, :] = jnp.exp2(s - ms[h]).astype(BF16)
            excess.append(jnp.max(s, axis=0, keepdims=True) - ms[h])

        def accumulate(h):
            acc = acc_scr[h]
            for i in range(nblk):
                acc = acc + _dot(values(blk + i, h), p_scr[h % P_SLOTS, i * tq:(i + 1) * tq, :])
            acc_scr[h] = acc

        softmax(0)
        for h in range(1, nh):
            softmax(h)
            accumulate(h - 1)
        accumulate(nh - 1)
        return functools.reduce(jnp.maximum, excess)

    ms = (jnp.full((1, tq), NEG_INF, F32),) * nh
    ms = two_pass_chunk(qi, ms, True)
    acc_new_scr[...] = acc_scr[...]
    worst = jnp.full((1, tq), NEG_INF, F32)
    done, width = 0, FAST_CHUNKS
    while width >= 1:
        trips = (qi - done) // width
        worst = lax.fori_loop(
            0, trips,
            lambda t, w, done=done, width=width: jnp.maximum(w, single_pass(done + t * width, width, ms)), worst)
        done, width = done + trips * width, width // 2

    @pl.when(jnp.max(worst) > REF_MARGIN)
    def _():
        acc_scr[...] = acc_new_scr[...]
        lax.fori_loop(0, qi, lambda j, c: two_pass_chunk(j, c, False), ms)

    ot = jnp.concatenate([acc_scr[h, :MLA_V_DIM, :] / acc_scr[h, MLA_V_DIM:MLA_V_DIM + 1, :] for h in range(nh)],
                         axis=0)
    o_ref[...] = ot.T.astype(BF16)


def _attention(q, k, vt, tq, nh):
    bsz, seq, _ = q.shape
    groups = MLA_HEADS // nh
    return pl.pallas_call(
        functools.partial(_attn_kernel, tq=tq, nh=nh),
        grid=(bsz, groups, seq // tq),
        in_specs=[pl.BlockSpec((None, tq, nh * LANE), lambda b, h, i: (b, i, h)),
                  pl.BlockSpec((None, seq, nh * LANE), lambda b, h, i: (b, 0, h)),
                  pl.BlockSpec((None, seq // tq, None, nh * VT_ROWS, tq), lambda b, h, i: (b, 0, h, 0, 0))],
        out_specs=pl.BlockSpec((None, tq, nh * MLA_V_DIM), lambda b, h, i: (b, i, h)),
        out_shape=jax.ShapeDtypeStruct((bsz, seq, MLA_HEADS * MLA_V_DIM), BF16),
        scratch_shapes=[pltpu.VMEM((nh, tq, tq), F32), pltpu.VMEM((P_SLOTS, FAST_CHUNKS * tq, tq), BF16),
                        pltpu.VMEM((nh, VT_ROWS, tq), F32), pltpu.VMEM((nh, VT_ROWS, tq), F32)],
        compiler_params=_params("parallel", "parallel", "arbitrary"),
        name="mla_attention",
    )(q, k, vt)


def _layer_tail_kernel(x_ref, ao_ref, so_ref, gate_ref, kx_ref, vx_ref, lng_ref, lnb_ref, w_oa_ref, w_o_ref,
                       ln1g_ref, ln1b_ref, w_xq_ref, w_xo_ref, ln2g_ref, ln2b_ref, w_up_ref, w_down_ref,
                       ln3g_ref, ln3b_ref, out_ref, h2_scr, *, x_scale, chunk):
    @pl.when(pl.program_id(0) == 0)
    def _():
        h2_scr[...] = jnp.zeros_like(h2_scr)

    h2_prev = h2_scr[...]
    hb_prev = h2_prev.astype(BF16)
    nchunk = w_up_ref.shape[-1] // chunk

    def mlp_units():
        ff = jnp.zeros_like(h2_prev)
        for c in range(nchunk):
            cols = slice(c * chunk, (c + 1) * chunk)
            a = jnp.maximum(_dot(hb_prev, w_up_ref[:, cols]), 0.0)
            a = (a * a).astype(BF16)
            yield None
            ff = ff + _dot(a, w_down_ref[cols, :])
            yield ff

    units = mlp_units()
    ff = [None]

    def issue(n):
        for _ in range(n):
            ff[0] = next(units, ff[0])

    h0 = _layer_norm(x_ref[...], lng_ref[...], lnb_ref[...])
    d = h0.shape[-1]
    a_out = _dot(ao_ref[...], w_oa_ref[...])
    issue(1)
    gate = gate_ref[...]
    mixed = gate[:, :d].astype(F32) * so_ref[...].astype(F32) + gate[:, d:].astype(F32) * a_out
    mix = _dot(mixed.astype(BF16), w_o_ref[...])
    issue(2)
    h1 = _layer_norm(DN_ALPHA * h0 + mix, ln1g_ref[...], ln1b_ref[...])
    qx = (_dot(h1.astype(BF16), w_xq_ref[...]) * x_scale).astype(BF16)
    issue(2)
    hd = d // XATTN_HEADS
    xa = jnp.zeros_like(h1)
    for hh in range(XATTN_HEADS):
        cols = slice(hh * hd, (hh + 1) * hd)
        s = _dot_nt(qx[:, cols], kx_ref[:, cols])
        issue(1)
        p = jnp.exp(s - jnp.max(s, axis=-1, keepdims=True))
        o = _dot(p.astype(BF16), vx_ref[:, cols]) / jnp.sum(p, axis=-1, keepdims=True)
        xa = xa + _dot(o.astype(BF16), w_xo_ref[cols, :])
    issue(2 * nchunk)
    h2 = _layer_norm(DN_ALPHA * h1 + xa, ln2g_ref[...], ln2b_ref[...])
    out_ref[...] = _layer_norm(DN_ALPHA * h2_prev + ff[0], ln3g_ref[...], ln3b_ref[...])
    h2_scr[...] = h2


def _layer_tail(x, ao, so, gate, kx, vx, ln_g, ln_b, w_oa, w_o, ln1_g, ln1_b, w_xq, w_xo, ln2_g, ln2_b,
                w_up, w_down, ln3_g, ln3_b, tm):
    bsz, seq, d = x.shape
    m = kx.shape[1]
    hidden = w_up.shape[-1]
    per_seq = seq // tm
    ntile = bsz * per_seq
    vec = lambda a: a.reshape(1, d)
    flat = lambda a: a.reshape(bsz * seq, a.shape[-1])
    cur = lambda i: jnp.minimum(i, ntile - 1)
    tok = lambda w: pl.BlockSpec((tm, w), lambda i: (cur(i), 0))
    per_b = pl.BlockSpec((None, m, d), lambda i: (cur(i) // per_seq, 0, 0))
    consts = [vec(ln_g), vec(ln_b), w_oa.astype(BF16), w_o.astype(BF16), vec(ln1_g), vec(ln1_b),
              w_xq.astype(BF16), w_xo.astype(BF16), vec(ln2_g), vec(ln2_b),
              w_up.astype(BF16), w_down.astype(BF16), vec(ln3_g), vec(ln3_b)]
    out = pl.pallas_call(
        functools.partial(_layer_tail_kernel, x_scale=(d // XATTN_HEADS) ** -0.5, chunk=min(hidden, MLP_CHUNK)),
        grid=(ntile + 1,),
        in_specs=[tok(d), tok(ao.shape[-1]), tok(d), tok(2 * d), per_b, per_b] + [_const_spec(c.shape) for c in consts],
        out_specs=pl.BlockSpec((tm, d), lambda i: (jnp.maximum(i - 1, 0), 0)),
        out_shape=jax.ShapeDtypeStruct((bsz * seq, d), F32),
        scratch_shapes=[pltpu.VMEM((tm, d), F32)],
        compiler_params=_params("arbitrary"),
        name="layer_tail",
    )(flat(x), flat(ao), flat(so), flat(gate), kx, vx, *consts)
    return out.reshape(bsz, seq, d)


def kernel(x, mem, positions, ln_in_g, ln_in_b, w_in, s5_lam_re, s5_lam_im, s5_log_dt, s5_b_re, s5_b_im,
           s5_c_re, s5_c_im, s5_d, w_glu, q_norm_g, w_uq, kv_norm_g, w_ukv, w_oa, w_o, ln1_g, ln1_b,
           w_xq, w_xk, w_xv, w_xo, ln2_g, ln2_b, w_up, w_down, ln3_g, ln3_b):
    bsz, seq, d = x.shape
    n = bsz * seq
    assert w_in.shape[0] == DEPTH == 1
    tm = min(seq, 512)
    tq = min(seq, 512)
    nh = 4
    tt = min(seq, 128)
    nsub = 2

    cos, sin = _rope_tables(positions)
    ones = jnp.ones((bsz, seq, MLA_NOPE_DIM), F32)
    zeros = jnp.zeros((bsz, seq, LANE - MLA_QK_DIM), F32)
    cos128 = jnp.concatenate([ones, cos, cos, zeros], axis=-1)
    sin128 = jnp.concatenate([0.0 * ones, sin, sin, zeros], axis=-1)

    assert tm == tq
    u, q, k, vt, gate = _in_proj(x, cos128, sin128, ln_in_g, ln_in_b, w_in[0],
                                 q_norm_g[0], w_uq[0], kv_norm_g[0], w_ukv[0], tm, nh)

    ab_re, ab_im, bb_re, bb_im = _s5_discretize(s5_lam_re[0], s5_lam_im[0], s5_log_dt[0], s5_b_re[0], s5_b_im[0])
    s_out = _s5_scan(u, ab_re, ab_im, bb_re, bb_im, s5_c_re[0], s5_c_im[0],
                     s5_d[0], w_glu[0], tt, nsub)

    a_o = _attention(q, k, vt, tq, nh)

    kx, vx = _mem_kv(mem, w_xk[0], w_xv[0])
    return _layer_tail(x, a_o, s_out, gate, kx, vx, ln_in_g, ln_in_b, w_oa[0], w_o[0], ln1_g[0], ln1_b[0],
                       w_xq[0], w_xo[0], ln2_g[0], ln2_b[0], w_up[0], w_down[0], ln3_g[0], ln3_b[0], tm)
```

```python
import functools
import math

import jax
import jax.numpy as jnp
from jax import lax
from jax.experimental import pallas as pl
from jax.experimental.pallas import tpu as pltpu

F32 = jnp.float32
BF16 = jnp.bfloat16

S5_GROUP_CH = 16
S5_STATE = 64
S5_MAX_RE = -1e-4
MLA_HEADS = 8
MLA_NOPE_DIM = 64
MLA_ROPE_DIM = 32
MLA_QK_DIM = MLA_NOPE_DIM + MLA_ROPE_DIM
MLA_V_DIM = 64
ROPE_THETA = 10000.0
XATTN_HEADS = 4
LN_EPS = 1e-5
RMS_EPS = 1e-6
NEG_INF = -1e30
DEPTH = 1
DN_ALPHA = (2.0 * DEPTH) ** 0.25

MAX_CHAINS = 4
MLP_CHUNK = 1024
QK_AHEAD = 2
FAST_CHUNKS = 8
P_SLOTS = 2
REF_MARGIN = 64.0
F32_SUBLANES = 8
BF16_SUBLANES = 16
VT_ROWS = MLA_V_DIM + BF16_SUBLANES
LANE = 128
VMEM_LIMIT = 56 * 1024 * 1024


def _const_spec(shape):
    nd = len(shape)
    return pl.BlockSpec(shape, lambda *_: (0,) * nd, pipeline_mode=pl.Buffered(1))


def _params(*sem, flags=None):
    return pltpu.CompilerParams(dimension_semantics=sem, vmem_limit_bytes=VMEM_LIMIT, flags=flags)


def _layer_norm(x, g, b):
    mu = jnp.mean(x, axis=-1, keepdims=True)
    xc = x - mu
    var = jnp.mean(xc * xc, axis=-1, keepdims=True)
    return xc * lax.rsqrt(var + LN_EPS) * g + b


def _rms_norm(x, g):
    return x * lax.rsqrt(jnp.mean(x * x, axis=-1, keepdims=True) + RMS_EPS) * g


def _gelu_exact(x):
    return 0.5 * x * (1.0 + lax.erf(x * (0.5 ** 0.5)))


def _dot(a, b):
    return jnp.dot(a, b, preferred_element_type=F32)


def _dot_nt(a, b):
    return lax.dot_general(a, b, (((1,), (1,)), ((), ())), preferred_element_type=F32)


def _s5_disc_kernel(lr_ref, li_ref, ldt_ref, br_ref, bi_ref, are_ref, aim_ref, bbr_ref, bbi_ref):
    lr = jnp.minimum(lr_ref[...], S5_MAX_RE)
    li = li_ref[...]
    dt = jnp.exp(ldt_ref[...])
    mag = jnp.exp(lr * dt)
    ang = li * dt
    ab_re = mag * jnp.cos(ang)
    ab_im = mag * jnp.sin(ang)
    den = lr * lr + li * li
    nr = ab_re - 1.0
    f_re = (nr * lr + ab_im * li) / den
    f_im = (ab_im * lr - nr * li) / den
    br = br_ref[...]
    bi = bi_ref[...]
    are_ref[...] = ab_re
    aim_ref[...] = ab_im
    bbr_ref[...] = f_re * br - f_im * bi
    bbi_ref[...] = f_re * bi + f_im * br


def _s5_discretize(lam_re, lam_im, log_dt, b_re, b_im):
    g, p = lam_re.shape
    h = b_re.shape[-1]
    n = g * p
    col = lambda a: a.reshape(n, 1)
    ldt = jnp.broadcast_to(log_dt[:, None], (g, p))
    outs = pl.pallas_call(
        _s5_disc_kernel,
        out_shape=[jax.ShapeDtypeStruct((n, 1), F32), jax.ShapeDtypeStruct((n, 1), F32),
                   jax.ShapeDtypeStruct((n, h), F32), jax.ShapeDtypeStruct((n, h), F32)],
        name="s5_discretize",
    )(col(lam_re), col(lam_im), col(ldt), b_re.reshape(n, h), b_im.reshape(n, h))
    ab_re, ab_im, bb_re, bb_im = outs
    return ab_re.reshape(g, p), ab_im.reshape(g, p), bb_re.reshape(g, p, h), bb_im.reshape(g, p, h)


def _rope_kernel(pos_ref, inv_ref, cos_ref, sin_ref):
    ang = pos_ref[...].astype(F32) * inv_ref[...]
    cos_ref[...] = jnp.cos(ang)
    sin_ref[...] = jnp.sin(ang)


def _rope_tables(positions):
    bsz, seq = positions.shape
    half = MLA_ROPE_DIM // 2
    per_row = LANE // half
    rows = bsz * seq // per_row
    inv = ROPE_THETA ** (-jnp.arange(0, MLA_ROPE_DIM, 2, dtype=F32) / MLA_ROPE_DIM)
    inv_row = jnp.tile(inv, per_row).reshape(1, LANE)
    pos_rep = jnp.repeat(positions.reshape(-1), half).reshape(rows, LANE)
    tr = min(rows, 1024)
    cos, sin = pl.pallas_call(
        _rope_kernel,
        grid=(rows // tr,),
        in_specs=[pl.BlockSpec((tr, LANE), lambda i: (i, 0)), _const_spec((1, LANE))],
        out_specs=[pl.BlockSpec((tr, LANE), lambda i: (i, 0))] * 2,
        out_shape=[jax.ShapeDtypeStruct((rows, LANE), F32)] * 2,
        compiler_params=_params("parallel"),
        name="rope_table",
    )(pos_rep, inv_row)
    return cos.reshape(bsz, seq, half), sin.reshape(bsz, seq, half)


def _mem_kv_kernel(mem_ref, wk_ref, wv_ref, k_ref, v_ref):
    m = mem_ref[...].astype(BF16)
    k_ref[...] = _dot(m, wk_ref[...]).astype(BF16)
    v_ref[...] = _dot(m, wv_ref[...]).astype(BF16)


def _mem_kv(mem, w_xk, w_xv):
    bsz, m, d = mem.shape
    return pl.pallas_call(
        _mem_kv_kernel,
        grid=(bsz,),
        in_specs=[pl.BlockSpec((None, m, d), lambda b: (b, 0, 0)), _const_spec((d, d)), _const_spec((d, d))],
        out_specs=[pl.BlockSpec((None, m, d), lambda b: (b, 0, 0))] * 2,
        out_shape=[jax.ShapeDtypeStruct((bsz, m, d), BF16)] * 2,
        compiler_params=_params("parallel"),
        name="mem_kv",
    )(mem, w_xk.astype(BF16), w_xv.astype(BF16))


def _in_proj_kernel(x_ref, cos_ref, sin_ref, lng_ref, lnb_ref, w_u_ref, w_cq_ref, w_ckv_ref, w_kr_ref,
                    w_g_ref, qg_ref, kvg_ref, w_q_ref, w_qs_ref, w_k_ref, w_vt_ref,
                    u_ref, q_ref, k_ref, vt_ref, gate_ref, *, q_scale, nh):
    hb = _layer_norm(x_ref[...], lng_ref[...], lnb_ref[...]).astype(BF16)
    cos = cos_ref[...]
    sin = sin_ref[...]
    gate_cols = gate_ref.shape[-1] // 4

    def gate_quarter(c):
        cols = slice(c * gate_cols, (c + 1) * gate_cols)
        gate_ref[:, cols] = jax.nn.sigmoid(_dot(hb, w_g_ref[:, cols])).astype(BF16)

    cq = _dot(hb, w_cq_ref[...])
    ckv = _dot(hb, w_ckv_ref[...])
    gate_quarter(0)
    cqn = _rms_norm(cq, qg_ref[...]).astype(BF16)
    qa = _dot(cqn, w_q_ref[...])
    qb = _dot(cqn, w_qs_ref[...])
    gate_quarter(1)
    ckvn = _rms_norm(ckv, kvg_ref[...]).astype(BF16)
    kn = _dot(ckvn, w_k_ref[...])
    kr2 = _dot(hb, w_kr_ref[...])
    k_rope = kr2[:, :LANE] * cos + kr2[:, LANE:] * sin
    vt = _dot_nt(w_vt_ref[...], ckvn)
    gate_quarter(2)
    gate_quarter(3)
    u_ref[...] = _dot(hb, w_u_ref[...])
    ones = jnp.ones((VT_ROWS - MLA_V_DIM, vt.shape[-1]), BF16)
    for h in range(MLA_HEADS):
        g, r0 = h // nh, (h % nh) * VT_ROWS
        vt_ref[g, r0:r0 + MLA_V_DIM, :] = vt[h * MLA_V_DIM:(h + 1) * MLA_V_DIM, :].astype(BF16)
        vt_ref[g, r0 + MLA_V_DIM:r0 + VT_ROWS, :] = ones
    for h in range(MLA_HEADS):
        sl = slice(h * LANE, (h + 1) * LANE)
        q_ref[:, sl] = ((qa[:, sl] * cos + qb[:, sl] * sin) * q_scale).astype(BF16)
        k_ref[:, sl] = (kn[:, sl] + k_rope).astype(BF16)


def _rotate_half_cols(w):
    half = MLA_ROPE_DIM // 2
    return jnp.concatenate([-w[..., half:], w[..., :half]], axis=-1)


def _head_slab(nope, rope):
    d, h, _ = nope.shape
    pad = jnp.zeros((d, h, LANE - MLA_QK_DIM), nope.dtype)
    return jnp.concatenate([nope, rope, pad], axis=-1).reshape(d, h * LANE)


def _in_proj(x, cos128, sin128, ln_g, ln_b, w_in, q_norm_g, w_uq, kv_norm_g, w_ukv, tm, nh):
    bsz, seq, d = x.shape
    groups = MLA_HEADS // nh
    s5w = q_rank = kv_rank = d // 4
    o1, o2, o3 = s5w, s5w + q_rank, s5w + q_rank + kv_rank
    o4 = o3 + MLA_ROPE_DIM
    hl = MLA_HEADS * LANE
    w_u = w_in[:, :o1].astype(BF16)
    w_cq = w_in[:, o1:o2].astype(BF16)
    w_ckv = w_in[:, o2:o3].astype(BF16)
    w_kr = w_in[:, o3:o4]
    zl = jnp.zeros((d, MLA_NOPE_DIM), F32)
    zr = jnp.zeros((d, LANE - MLA_QK_DIM), F32)
    w_kr2 = jnp.concatenate([zl, w_kr, zr, zl, _rotate_half_cols(w_kr), zr], axis=-1).astype(BF16)
    w_g = w_in[:, o4:].astype(BF16)
    wq3 = w_uq.reshape(q_rank, MLA_HEADS, MLA_QK_DIM)
    wq_nope, wq_rope = wq3[..., :MLA_NOPE_DIM], wq3[..., MLA_NOPE_DIM:]
    w_q = _head_slab(wq_nope, wq_rope).astype(BF16)
    w_qs = _head_slab(jnp.zeros_like(wq_nope), _rotate_half_cols(wq_rope)).astype(BF16)
    wkv3 = w_ukv.reshape(kv_rank, MLA_HEADS, MLA_NOPE_DIM + MLA_V_DIM)
    w_k = _head_slab(wkv3[..., :MLA_NOPE_DIM],
                     jnp.zeros((kv_rank, MLA_HEADS, MLA_ROPE_DIM), F32)).astype(BF16)
    w_vt = wkv3[..., MLA_NOPE_DIM:].reshape(kv_rank, MLA_HEADS * MLA_V_DIM).T.astype(BF16)
    row = lambda w: pl.BlockSpec((None, tm, w), lambda b, i: (b, i, 0))
    tok = lambda w, dt: jax.ShapeDtypeStruct((bsz, seq, w), dt)
    consts = [ln_g.reshape(1, d), ln_b.reshape(1, d), w_u, w_cq, w_ckv, w_kr2, w_g,
              q_norm_g.reshape(1, q_rank), kv_norm_g.reshape(1, kv_rank), w_q, w_qs, w_k, w_vt]
    return pl.pallas_call(
        functools.partial(_in_proj_kernel, q_scale=MLA_QK_DIM ** -0.5 * math.log2(math.e), nh=nh),
        grid=(bsz, seq // tm),
        in_specs=[row(d), row(LANE), row(LANE)] + [_const_spec(c.shape) for c in consts],
        out_specs=[row(s5w), row(hl), row(hl),
                   pl.BlockSpec((None, None, groups, nh * VT_ROWS, tm), lambda b, i: (b, i, 0, 0, 0)), row(2 * d)],
        out_shape=[tok(s5w, F32), tok(hl, BF16), tok(hl, BF16),
                   jax.ShapeDtypeStruct((bsz, seq // tm, groups, nh * VT_ROWS, tm), BF16), tok(2 * d, BF16)],
        compiler_params=_params("parallel", "parallel"),
        name="in_proj",
    )(x, cos128, sin128, *consts)


def _s5_scan_kernel(u_ref, bblk_ref, are_ref, aim_ref, cblk_ref, d_ref, wglu_ref, out_ref,
                    hbuf, st_re, st_im, *, bsz, tt, nsub, nstate):
    @pl.when(pl.program_id(0) == 0)
    def _():
        st_re[...] = jnp.zeros_like(st_re)
        st_im[...] = jnp.zeros_like(st_im)

    width = u_ref.shape[-1]
    nt = nstate // LANE
    ar = [jnp.broadcast_to(are_ref[:, c * LANE:(c + 1) * LANE], (bsz, LANE)) for c in range(nt)]
    ai = [jnp.broadcast_to(aim_ref[:, c * LANE:(c + 1) * LANE], (bsz, LANE)) for c in range(nt)]

    def project_in(sc):
        u = jnp.swapaxes(u_ref[:, sc * tt:(sc + 1) * tt, :], 0, 1).reshape(tt * bsz, width)
        bu = _dot(u.astype(BF16), bblk_ref[...])
        for c in range(2 * nt):
            hbuf[sc, c] = bu[:, c * LANE:(c + 1) * LANE]
        return u

    def recurrence(sc, hr, hi):
        for t in range(tt):
            rows = slice(t * bsz, (t + 1) * bsz)
            nr, ni = [], []
            for c in range(nt):
                r = ar[c] * hr[c] - ai[c] * hi[c] + hbuf[sc, c, rows, :]
                i = ar[c] * hi[c] + ai[c] * hr[c] + hbuf[sc, nt + c, rows, :]
                hbuf[sc, c, rows, :] = r
                hbuf[sc, nt + c, rows, :] = i
                nr.append(r)
                ni.append(i)
            hr, hi = nr, ni
        return hr, hi

    def project_out(sc, u):
        half = tt * bsz // 2
        ys = []
        for r in range(2):
            rows = slice(r * half, (r + 1) * half)
            hs = jnp.concatenate([hbuf[sc, c, rows, :].astype(BF16) for c in range(2 * nt)], axis=-1)
            ys.append(_dot(hs, cblk_ref[...]))
        y = jnp.concatenate(ys, axis=0) + d_ref[...] * u
        y = jnp.swapaxes(y.reshape(tt, bsz, width), 0, 1).reshape(bsz * tt, width)
        return _gelu_exact(y).astype(BF16)

    def gate_out(sc, yg):
        z = _dot(yg, wglu_ref[...])
        dm = z.shape[-1] // 2
        s_out = z[:, :dm] * jax.nn.sigmoid(z[:, dm:])
        out_ref[:, sc * tt:(sc + 1) * tt, :] = s_out.reshape(bsz, tt, dm).astype(BF16)

    us = [project_in(sc) for sc in range(nsub)]
    hr = [st_re[c] for c in range(nt)]
    hi = [st_im[c] for c in range(nt)]
    pending = None
    for sc in range(nsub):
        hr, hi = recurrence(sc, hr, hi)
        yg = project_out(sc, us[sc])
        if pending is not None:
            gate_out(*pending)
        pending = (sc, yg)
    gate_out(*pending)
    for c in range(nt):
        st_re[c] = hr[c]
        st_im[c] = hi[c]


def _s5_scan(u3, ab_re, ab_im, bb_re, bb_im, c_re, c_im, d_skip, w_glu, tt, nsub):
    bsz, seq, width = u3.shape
    g, p, h = bb_re.shape
    nstate = g * p
    eye = jnp.eye(g, dtype=F32)
    blk_b = lambda bb: jnp.einsum('gph,gk->ghkp', bb, eye).reshape(g * h, nstate)
    bblk = jnp.concatenate([blk_b(bb_re), blk_b(bb_im)], axis=1).astype(BF16)
    blk_c = lambda c: jnp.einsum('ghp,gk->gpkh', c, eye).reshape(nstate, g * h)
    cblk = jnp.concatenate([blk_c(c_re), -blk_c(c_im)], axis=0).astype(BF16)
    dm2 = w_glu.shape[-1]
    consts = [bblk, ab_re.reshape(1, nstate), ab_im.reshape(1, nstate), cblk,
              d_skip.reshape(1, width), w_glu.astype(BF16)]
    return pl.pallas_call(
        functools.partial(_s5_scan_kernel, bsz=bsz, tt=tt, nsub=nsub, nstate=nstate),
        grid=(seq // (tt * nsub),),
        in_specs=[pl.BlockSpec((bsz, tt * nsub, width), lambda i: (0, i, 0))] + [_const_spec(c.shape) for c in consts],
        out_specs=pl.BlockSpec((bsz, tt * nsub, dm2 // 2), lambda i: (0, i, 0)),
        out_shape=jax.ShapeDtypeStruct((bsz, seq, dm2 // 2), BF16),
        scratch_shapes=[pltpu.VMEM((nsub, 2 * nstate // LANE, bsz * tt, LANE), F32),
                        pltpu.VMEM((nstate // LANE, bsz, LANE), F32),
                        pltpu.VMEM((nstate // LANE, bsz, LANE), F32)],
        compiler_params=_params("arbitrary"),
        name="s5_scan",
    )(u3, *consts)


def _attn_kernel(q_ref, k_ref, vt_ref, o_ref, s_scr, p_scr, acc_scr, acc_new_scr, *, tq, nh):
    qi = pl.program_id(2)
    heads = [slice(h * LANE, (h + 1) * LANE) for h in range(nh)]
    acc_scr[...] = jnp.zeros_like(acc_scr)

    def scores(blk, h, masked=False, nblk=1):
        rows = pl.ds(pl.multiple_of(blk * tq, tq), nblk * tq)
        s = _dot_nt(k_ref[rows, heads[h]], q_ref[:, heads[h]])
        if masked:
            keep = (lax.broadcasted_iota(jnp.int32, (tq, tq), 0) <= lax.broadcasted_iota(jnp.int32, (tq, tq), 1))
            s = jnp.where(keep, s, NEG_INF)
        return s

    def values(blk, h):
        return vt_ref[blk, h * VT_ROWS:(h + 1) * VT_ROWS, :]

    def two_pass_chunk(blk, ms, masked):
        for h in range(min(QK_AHEAD, nh)):
            s_scr[h] = scores(blk, h, masked)
        new_ms = []
        for h in range(nh):
            if h + QK_AHEAD < nh:
                s_scr[h + QK_AHEAD] = scores(blk, h + QK_AHEAD, masked)
            tiles = [s_scr[h, r * F32_SUBLANES:(r + 1) * F32_SUBLANES, :] for r in range(tq // F32_SUBLANES)]
            part = tiles[:MAX_CHAINS]
            for r, t in enumerate(tiles[MAX_CHAINS:]):
                part[r % MAX_CHAINS] = jnp.maximum(part[r % MAX_CHAINS], t)
            m8 = functools.reduce(jnp.maximum, part)
            m_new = jnp.maximum(ms[h], jnp.max(m8, axis=0, keepdims=True))
            for r in range(tq // BF16_SUBLANES):
                sl = slice(r * BF16_SUBLANES, (r + 1) * BF16_SUBLANES)
                p_scr[h % P_SLOTS, sl, :] = jnp.exp2(s_scr[h, sl, :] - m_new).astype(BF16)
            acc_scr[h] = jnp.exp2(ms[h] - m_new) * acc_scr[h] + _dot(values(blk, h), p_scr[h % P_SLOTS, :tq, :])
            new_ms.append(m_new)
        return tuple(new_ms)

    def single_pass(blk, nblk, ms):
        excess = []

        def softmax(h):
            s = scores(blk, h, nblk=nblk)
            p_scr[h % P_SLOTS, :nblk * tq, :] = jnp.exp2(s - ms[h]).astype(BF16)
            excess.append(jnp.max(s, axis=0, keepdims=True) - ms[h])

        def accumulate(h):
            acc = acc_scr[h]
            for i in range(nblk):
                acc = acc + _dot(values(blk + i, h), p_scr[h % P_SLOTS, i * tq:(i + 1) * tq, :])
            acc_new_scr[h] = acc

        softmax(0)
        for h in range(1, nh):
            softmax(h)
            accumulate(h - 1)
        accumulate(nh - 1)
        return jnp.max(functools.reduce(jnp.maximum, excess))

    def step(blk, nblk, ms):
        worst = single_pass(blk, nblk, ms)

        def redo(ms):
            return lax.fori_loop(0, nblk, lambda i, c: two_pass_chunk(blk + i, c, False), ms)

        def commit(ms):
            acc_scr[...] = acc_new_scr[...]
            return ms

        return lax.cond(worst > REF_MARGIN, redo, commit, ms)

    ms = (jnp.full((1, tq), NEG_INF, F32),) * nh
    ms = two_pass_chunk(qi, ms, True)
    done, width = 0, FAST_CHUNKS
    while width >= 1:
        trips = (qi - done) // width
        ms = lax.fori_loop(0, trips, lambda t, c, done=done, width=width: step(done + t * width, width, c), ms)
        done, width = done + trips * width, width // 2
    ot = jnp.concatenate([acc_scr[h, :MLA_V_DIM, :] / acc_scr[h, MLA_V_DIM:MLA_V_DIM + 1, :] for h in range(nh)],
                         axis=0)
    o_ref[...] = ot.T.astype(BF16)


def _attention(q, k, vt, tq, nh):
    bsz, seq, _ = q.shape
    groups = MLA_HEADS // nh
    return pl.pallas_call(
        functools.partial(_attn_kernel, tq=tq, nh=nh),
        grid=(bsz, groups, seq // tq),
        in_specs=[pl.BlockSpec((None, tq, nh * LANE), lambda b, h, i: (b, i, h)),
                  pl.BlockSpec((None, seq, nh * LANE), lambda b, h, i: (b, 0, h)),
                  pl.BlockSpec((None, seq // tq, None, nh * VT_ROWS, tq), lambda b, h, i: (b, 0, h, 0, 0))],
        out_specs=pl.BlockSpec((None, tq, nh * MLA_V_DIM), lambda b, h, i: (b, i, h)),
        out_shape=jax.ShapeDtypeStruct((bsz, seq, MLA_HEADS * MLA_V_DIM), BF16),
        scratch_shapes=[pltpu.VMEM((nh, tq, tq), F32), pltpu.VMEM((P_SLOTS, FAST_CHUNKS * tq, tq), BF16),
                        pltpu.VMEM((nh, VT_ROWS, tq), F32), pltpu.VMEM((nh, VT_ROWS, tq), F32)],
        compiler_params=_params("parallel", "parallel", "arbitrary"),
        name="mla_attention",
    )(q, k, vt)


def _layer_tail_kernel(x_ref, ao_ref, so_ref, gate_ref, kx_ref, vx_ref, lng_ref, lnb_ref, w_oa_ref, w_o_ref,
                       ln1g_ref, ln1b_ref, w_xq_ref, w_xo_ref, ln2g_ref, ln2b_ref, w_up_ref, w_down_ref,
                       ln3g_ref, ln3b_ref, out_ref, h2_scr, *, x_scale, chunk):
    @pl.when(pl.program_id(0) == 0)
    def _():
        h2_scr[...] = jnp.zeros_like(h2_scr)

    h2_prev = h2_scr[...]
    hb_prev = h2_prev.astype(BF16)
    nchunk = w_up_ref.shape[-1] // chunk

    def mlp_units():
        ff = jnp.zeros_like(h2_prev)
        for c in range(nchunk):
            cols = slice(c * chunk, (c + 1) * chunk)
            a = jnp.maximum(_dot(hb_prev, w_up_ref[:, cols]), 0.0)
            a = (a * a).astype(BF16)
            yield None
            ff = ff + _dot(a, w_down_ref[cols, :])
            yield ff

    units = mlp_units()
    ff = [None]

    def issue(n):
        for _ in range(n):
            ff[0] = next(units, ff[0])

    h0 = _layer_norm(x_ref[...], lng_ref[...], lnb_ref[...])
    d = h0.shape[-1]
    a_out = _dot(ao_ref[...], w_oa_ref[...])
    issue(1)
    gate = gate_ref[...]
    mixed = gate[:, :d].astype(F32) * so_ref[...].astype(F32) + gate[:, d:].astype(F32) * a_out
    mix = _dot(mixed.astype(BF16), w_o_ref[...])
    issue(2)
    h1 = _layer_norm(DN_ALPHA * h0 + mix, ln1g_ref[...], ln1b_ref[...])
    qx = (_dot(h1.astype(BF16), w_xq_ref[...]) * x_scale).astype(BF16)
    issue(2)
    hd = d // XATTN_HEADS
    xa = jnp.zeros_like(h1)
    for hh in range(XATTN_HEADS):
        cols = slice(hh * hd, (hh + 1) * hd)
        s = _dot_nt(qx[:, cols], kx_ref[:, cols])
        issue(1)
        p = jnp.exp(s - jnp.max(s, axis=-1, keepdims=True))
        o = _dot(p.astype(BF16), vx_ref[:, cols]) / jnp.sum(p, axis=-1, keepdims=True)
        xa = xa + _dot(o.astype(BF16), w_xo_ref[cols, :])
    issue(2 * nchunk)
    h2 = _layer_norm(DN_ALPHA * h1 + xa, ln2g_ref[...], ln2b_ref[...])
    out_ref[...] = _layer_norm(DN_ALPHA * h2_prev + ff[0], ln3g_ref[...], ln3b_ref[...])
    h2_scr[...] = h2


def _layer_tail(x, ao, so, gate, kx, vx, ln_g, ln_b, w_oa, w_o, ln1_g, ln1_b, w_xq, w_xo, ln2_g, ln2_b,
                w_up, w_down, ln3_g, ln3_b, tm):
    bsz, seq, d = x.shape
    m = kx.shape[1]
    hidden = w_up.shape[-1]
    per_seq = seq // tm
    ntile = bsz * per_seq
    vec = lambda a: a.reshape(1, d)
    flat = lambda a: a.reshape(bsz * seq, a.shape[-1])
    cur = lambda i: jnp.minimum(i, ntile - 1)
    tok = lambda w: pl.BlockSpec((tm, w), lambda i: (cur(i), 0))
    per_b = pl.BlockSpec((None, m, d), lambda i: (cur(i) // per_seq, 0, 0))
    consts = [vec(ln_g), vec(ln_b), w_oa.astype(BF16), w_o.astype(BF16), vec(ln1_g), vec(ln1_b),
              w_xq.astype(BF16), w_xo.astype(BF16), vec(ln2_g), vec(ln2_b),
              w_up.astype(BF16), w_down.astype(BF16), vec(ln3_g), vec(ln3_b)]
    out = pl.pallas_call(
        functools.partial(_layer_tail_kernel, x_scale=(d // XATTN_HEADS) ** -0.5, chunk=min(hidden, MLP_CHUNK)),
        grid=(ntile + 1,),
        in_specs=[tok(d), tok(ao.shape[-1]), tok(d), tok(2 * d), per_b, per_b] + [_const_spec(c.shape) for c in consts],
        out_specs=pl.BlockSpec((tm, d), lambda i: (jnp.maximum(i - 1, 0), 0)),
        out_shape=jax.ShapeDtypeStruct((bsz * seq, d), F32),
        scratch_shapes=[pltpu.VMEM((tm, d), F32)],
        compiler_params=_params("arbitrary"),
        name="layer_tail",
    )(flat(x), flat(ao), flat(so), flat(gate), kx, vx, *consts)
    return out.reshape(bsz, seq, d)


def kernel(x, mem, positions, ln_in_g, ln_in_b, w_in, s5_lam_re, s5_lam_im, s5_log_dt, s5_b_re, s5_b_im,
           s5_c_re, s5_c_im, s5_d, w_glu, q_norm_g, w_uq, kv_norm_g, w_ukv, w_oa, w_o, ln1_g, ln1_b,
           w_xq, w_xk, w_xv, w_xo, ln2_g, ln2_b, w_up, w_down, ln3_g, ln3_b):
    bsz, seq, d = x.shape
    n = bsz * seq
    assert w_in.shape[0] == DEPTH == 1
    tm = min(seq, 512)
    tq = min(seq, 512)
    nh = 4
    tt = min(seq, 128)
    nsub = 2

    cos, sin = _rope_tables(positions)
    ones = jnp.ones((bsz, seq, MLA_NOPE_DIM), F32)
    zeros = jnp.zeros((bsz, seq, LANE - MLA_QK_DIM), F32)
    cos128 = jnp.concatenate([ones, cos, cos, zeros], axis=-1)
    sin128 = jnp.concatenate([0.0 * ones, sin, sin, zeros], axis=-1)

    assert tm == tq
    u, q, k, vt, gate = _in_proj(x, cos128, sin128, ln_in_g, ln_in_b, w_in[0],
                                 q_norm_g[0], w_uq[0], kv_norm_g[0], w_ukv[0], tm, nh)

    ab_re, ab_im, bb_re, bb_im = _s5_discretize(s5_lam_re[0], s5_lam_im[0], s5_log_dt[0], s5_b_re[0], s5_b_im[0])
    s_out = _s5_scan(u, ab_re, ab_im, bb_re, bb_im, s5_c_re[0], s5_c_im[0],
                     s5_d[0], w_glu[0], tt, nsub)

    a_o = _attention(q, k, vt, tq, nh)

    kx, vx = _mem_kv(mem, w_xk[0], w_xv[0])
    return _layer_tail(x, a_o, s_out, gate, kx, vx, ln_in_g, ln_in_b, w_oa[0], w_o[0], ln1_g[0], ln1_b[0],
                       w_xq[0], w_xo[0], ln2_g[0], ln2_b[0], w_up[0], w_down[0], ln3_g[0], ln3_b[0], tm)
```
